```python
import jax, jax.numpy as jnp
from jax import lax
import numpy as np

D_MODEL = 1024
BATCH = 1
SEQ = 16384
DEPTH = 2
DEC_BATCH = 2
DEC_SEQ = 16384
PAST_LEN = 128

PLE_DIM = 256
POOL_GROUPS = 4
POOL_GROUP_DIM = D_MODEL // 8
POOL_WIDTH = POOL_GROUPS * POOL_GROUP_DIM
POOL_WINDOWS = (2, 4, 8, 16)
RET_HEADS = 4
RET_DK = D_MODEL // 8
RET_DV = 2 * RET_DK
RET_QK_WIDTH = RET_HEADS * RET_DK
RET_V_WIDTH = RET_HEADS * RET_DV
N_BRANCHES = 2
CHUNK = 128
D_FF = -(-8 * D_MODEL // (3 * 256)) * 256
IN_WIDTH = POOL_WIDTH + 2 * RET_QK_WIDTH + 2 * RET_V_WIDTH + N_BRANCHES * D_MODEL
IN_SPLITS = (POOL_WIDTH,
             POOL_WIDTH + RET_QK_WIDTH,
             POOL_WIDTH + 2 * RET_QK_WIDTH,
             POOL_WIDTH + 2 * RET_QK_WIDTH + RET_V_WIDTH,
             POOL_WIDTH + 2 * RET_QK_WIDTH + 2 * RET_V_WIDTH)
ROPE_BASE = 10000.0
RMS_EPS = 1e-6
GN_EPS = 1e-5

kernel_name = "hybrid_pool_retention_encoder"


def rmsnorm(x, g):
    xf = x.astype(jnp.float32)
    y = xf * lax.rsqrt(jnp.mean(xf * xf, axis=-1, keepdims=True) + RMS_EPS)
    return (y * g.astype(jnp.float32)).astype(x.dtype)


def rotary(x, pos):
    half = x.shape[-1] // 2
    inv = 1.0 / (ROPE_BASE ** (jnp.arange(half, dtype=jnp.float32) / half))
    ang = pos.astype(jnp.float32)[:, None] * inv[None, :]
    cos = jnp.cos(ang)[None, :, None, :]
    sin = jnp.sin(ang)[None, :, None, :]
    x1, x2 = x[..., :half], x[..., half:]
    return jnp.concatenate([x1 * cos - x2 * sin, x1 * sin + x2 * cos], axis=-1)


def pool_mixer(xa, pool_w, pool_scale):
    B, L, _ = xa.shape
    xf = xa.astype(jnp.float32)
    cs = jnp.concatenate([jnp.zeros((B, 1, POOL_WIDTH), jnp.float32),
                          lax.cumsum(xf, axis=1)], axis=1)
    t = jnp.arange(L)
    outs = []
    for g, w in enumerate(POOL_WINDOWS):
        sl = slice(g * POOL_GROUP_DIM, (g + 1) * POOL_GROUP_DIM)
        lo = jnp.clip(t - w // 2, 0, L)
        hi = jnp.clip(t + w // 2, 0, L)
        csg = cs[..., sl]
        cnt = (hi - lo).astype(jnp.float32)[None, :, None]
        mean = (jnp.take(csg, hi, axis=1) - jnp.take(csg, lo, axis=1)) / cnt
        outs.append(jnp.einsum('blc,cd->bld', mean - xf[..., sl], pool_w[g].astype(jnp.float32)))
    y = jnp.concatenate(outs, axis=-1) * pool_scale.astype(jnp.float32)
    return y.astype(xa.dtype)


def retention_direction(q, k, v, log_g, inclusive):
    B, L, H, _ = q.shape
    N = L // CHUNK
    idx = jnp.arange(CHUNK, dtype=jnp.float32)
    diff = idx[:, None] - idx[None, :]
    mask = (diff >= 0) if inclusive else (diff > 0)
    dmask = jnp.where(mask[None], jnp.exp(log_g[:, None, None] * jnp.maximum(diff, 0.0)[None]), 0.0)
    q_decay = jnp.exp((idx[:, None] + 1.0) * log_g[None, :])[..., None]
    k_decay = jnp.exp((CHUNK - 1.0 - idx)[:, None] * log_g[None, :])[..., None]
    chunk_decay = jnp.exp(CHUNK * log_g)[None, :, None, None]

    def to_chunks(a):
        return a.reshape(B, N, CHUNK, H, a.shape[-1]).transpose(1, 0, 2, 3, 4)

    def step(S, inp):
        qc, kc, vc = inp
        scores = jnp.einsum('bihd,bjhd->bhij', qc, kc) * dmask[None]
        intra = jnp.einsum('bhij,bjhe->bihe', scores, vc)
        inter = jnp.einsum('bihd,bhde->bihe', qc * q_decay, S)
        S = S * chunk_decay + jnp.einsum('bjhd,bjhe->bhde', kc * k_decay, vc)
        return S, intra + inter

    S0 = jnp.zeros((B, H, q.shape[-1], v.shape[-1]), jnp.float32)
    _, ys = lax.scan(step, S0, (to_chunks(q), to_chunks(k), to_chunks(v)))
    return ys.transpose(1, 0, 2, 3, 4).reshape(B, L, H, v.shape[-1])


def retention_mixer(xq, xk, xv, xg, decay_logit):
    B, L, _ = xq.shape
    pos = jnp.arange(L)
    q = rotary(xq.astype(jnp.float32).reshape(B, L, RET_HEADS, RET_DK), pos)
    k = rotary(xk.astype(jnp.float32).reshape(B, L, RET_HEADS, RET_DK), pos) * (RET_DK ** -0.5)
    v = xv.astype(jnp.float32).reshape(B, L, RET_HEADS, RET_DV)
    log_g = jax.nn.log_sigmoid(decay_logit.astype(jnp.float32))
    fwd = retention_direction(q, k, v, log_g[0], True)
    bwd = jnp.flip(retention_direction(jnp.flip(q, 1), jnp.flip(k, 1), jnp.flip(v, 1), log_g[1], False), 1)
    y = fwd + bwd
    mu = jnp.mean(y, axis=-1, keepdims=True)
    yc = y - mu
    yn = yc * lax.rsqrt(jnp.mean(yc * yc, axis=-1, keepdims=True) + GN_EPS)
    out = yn.reshape(B, L, RET_V_WIDTH) * jax.nn.silu(xg.astype(jnp.float32))
    return out.astype(xq.dtype)


def layer(x, p_i, g_mix, w_in, pool_w, pool_scale, decay_logit, w_pool_out, w_ret_out, w_o,
          g_ffn, w_ffn_in, w_ffn_out, g_ple, w_ple_gate, w_ple_proj):
    B, L, D = x.shape
    h = rmsnorm(x, g_mix)
    z = jnp.einsum('bld,de->ble', h, w_in)
    xa, xq, xk, xv, xg, zgate = jnp.split(z, list(IN_SPLITS), axis=-1)
    a = jnp.einsum('blc,cd->bld', pool_mixer(xa, pool_w, pool_scale), w_pool_out)
    r = jnp.einsum('blc,cd->bld', retention_mixer(xq, xk, xv, xg, decay_logit), w_ret_out)
    gates = jax.nn.sigmoid(zgate.astype(jnp.float32)).reshape(B, L, N_BRANCHES, D).astype(x.dtype)
    merged = gates[:, :, 0, :] * a + gates[:, :, 1, :] * r
    x = x + jnp.einsum('bld,de->ble', merged, w_o)
    h2 = rmsnorm(x, g_ffn)
    gt, up = jnp.split(jnp.einsum('bld,df->blf', h2, w_ffn_in), 2, axis=-1)
    x = x + jnp.einsum('blf,fd->bld', jax.nn.silu(gt) * up, w_ffn_out)
    ple_gate = jax.nn.sigmoid(jnp.einsum('bld,de->ble', rmsnorm(x, g_ple), w_ple_gate).astype(jnp.float32)).astype(x.dtype)
    x = x + jnp.einsum('blp,pd->bld', p_i, w_ple_proj) * ple_gate
    return x


def trunk(x, p, g_mix, w_in, pool_w, pool_scale, decay_logit, w_pool_out, w_ret_out, w_o,
          g_ffn, w_ffn_in, w_ffn_out, g_ple, w_ple_gate, w_ple_proj, g_final):
    for i in range(DEPTH):
        x = layer(x, p[i], g_mix[i], w_in[i], pool_w[i], pool_scale[i], decay_logit[i],
                  w_pool_out[i], w_ret_out[i], w_o[i], g_ffn[i], w_ffn_in[i], w_ffn_out[i],
                  g_ple[i], w_ple_gate[i], w_ple_proj[i])
    return rmsnorm(x, g_final)


def setup_inputs(seed: int = 0) -> dict:
    key = jax.random.key(seed)
    ks = jax.random.split(key, 24)
    f32 = jnp.float32

    def nrm(k, shape, fan_in, scale=1.0):
        return jax.random.normal(k, shape, f32) * (scale * fan_in ** -0.5)

    def gain(k, shape):
        return 1.0 + 0.02 * jax.random.normal(k, shape, f32)

    base_logit = jnp.log(2.0 ** (5.0 + jnp.arange(RET_HEADS, dtype=f32)) - 1.0)
    decay_logit = base_logit[None, None, :] + 0.1 * jax.random.normal(ks[6], (DEPTH, 2, RET_HEADS), f32)
    return {
        "x_prompt": jax.random.normal(ks[0], (BATCH, SEQ, D_MODEL), f32),
        "x_sample": jax.random.normal(ks[1], (DEC_BATCH, DEC_SEQ, D_MODEL), f32),
        "p_prompt": jax.random.normal(ks[2], (DEPTH, BATCH, SEQ, PLE_DIM), f32),
        "p_sample": jax.random.normal(ks[3], (DEPTH, DEC_BATCH, DEC_SEQ, PLE_DIM), f32),
        "g_mix": gain(ks[4], (DEPTH, D_MODEL)),
        "w_in": nrm(ks[5], (DEPTH, D_MODEL, IN_WIDTH), D_MODEL),
        "pool_w": nrm(ks[7], (DEPTH, POOL_GROUPS, POOL_GROUP_DIM, POOL_GROUP_DIM), POOL_GROUP_DIM),
        "pool_scale": gain(ks[8], (DEPTH, POOL_WIDTH)),
        "decay_logit": decay_logit,
        "w_pool_out": nrm(ks[9], (DEPTH, POOL_WIDTH, D_MODEL), POOL_WIDTH),
        "w_ret_out": nrm(ks[10], (DEPTH, RET_V_WIDTH, D_MODEL), RET_V_WIDTH),
        "w_o": nrm(ks[11], (DEPTH, D_MODEL, D_MODEL), D_MODEL, 0.5),
        "g_ffn": gain(ks[12], (DEPTH, D_MODEL)),
        "w_ffn_in": nrm(ks[13], (DEPTH, D_MODEL, 2 * D_FF), D_MODEL),
        "w_ffn_out": nrm(ks[14], (DEPTH, D_FF, D_MODEL), D_FF, 0.5),
        "g_ple": gain(ks[15], (DEPTH, D_MODEL)),
        "w_ple_gate": nrm(ks[16], (DEPTH, D_MODEL, D_MODEL), D_MODEL),
        "w_ple_proj": nrm(ks[17], (DEPTH, PLE_DIM, D_MODEL), PLE_DIM, 0.5),
        "g_final": gain(ks[18], (D_MODEL,)),
    }


def reference(x_prompt, x_sample, p_prompt, p_sample, g_mix, w_in, pool_w, pool_scale, decay_logit,
              w_pool_out, w_ret_out, w_o, g_ffn, w_ffn_in, w_ffn_out, g_ple, w_ple_gate, w_ple_proj,
              g_final):
    y_prompt = trunk(x_prompt, p_prompt, g_mix, w_in, pool_w, pool_scale, decay_logit, w_pool_out,
                     w_ret_out, w_o, g_ffn, w_ffn_in, w_ffn_out, g_ple, w_ple_gate, w_ple_proj, g_final)
    y_sample = trunk(x_sample, p_sample, g_mix, w_in, pool_w, pool_scale, decay_logit, w_pool_out,
                     w_ret_out, w_o, g_ffn, w_ffn_in, w_ffn_out, g_ple, w_ple_gate, w_ple_proj, g_final)
    return (y_prompt, y_sample)
```

```python
import functools

import jax
import jax.numpy as jnp
from jax import lax
from jax.experimental import pallas as pl
from jax.experimental.pallas import tpu as pltpu

F32 = jnp.float32
BF16 = jnp.bfloat16

D_MODEL = 1024
PLE_DIM = 256
POOL_GROUPS = 4
POOL_GROUP_DIM = 128
POOL_WIDTH = 512
POOL_WINDOWS = (2, 4, 8, 16)
HEADS = 4
DK = 128
DV = 256
QK_WIDTH = HEADS * DK
V_WIDTH = HEADS * DV
D_FF = 2816
IN_WIDTH = POOL_WIDTH + 2 * QK_WIDTH + 2 * V_WIDTH + 2 * D_MODEL
ROPE_BASE = 10000.0
RMS_EPS = 1e-6
GN_EPS = 1e-5

OFF_Q = POOL_WIDTH
OFF_K = OFF_Q + QK_WIDTH
OFF_V = OFF_K + QK_WIDTH
OFF_G = OFF_V + V_WIDTH
OFF_GATE = OFF_G + V_WIDTH

HALO = 16
T_MIX = 256
T_FFN = 512
VMEM_LIMIT = 56 * 1024 * 1024


def _rms(x, g):
    return x * lax.rsqrt(jnp.mean(x * x, axis=-1, keepdims=True) + RMS_EPS) * g


def _rot(x, c, s):
    return x * c + pltpu.roll(x, DK // 2, 1) * s


def _dot(a, b):
    return jnp.dot(a, b, preferred_element_type=F32)


def _init_decay_tables(dl_ref, mask_ref, qd_ref, kd_ref, cd_ref, chunk, fwd):
    scale = DK ** -0.5
    ri = lax.broadcasted_iota(jnp.int32, (chunk, chunk), 0).astype(F32)
    ci = lax.broadcasted_iota(jnp.int32, (chunk, chunk), 1).astype(F32)
    r128 = lax.broadcasted_iota(jnp.int32, (chunk, DK), 0).astype(F32)
    for h in range(HEADS):
        x = dl_ref[h]
        lg = jnp.minimum(x, 0.0) - jnp.log1p(jnp.exp(-jnp.abs(x)))
        lgc = lg[:, :chunk]
        lgk = lg[:, :DK]
        if fwd:
            diff = ri - ci
            keep = diff >= 0.0
            qd = jnp.exp((r128 + 1.0) * lgk)
            kd = jnp.exp((chunk - 1.0 - r128) * lgk)
        else:
            diff = ci - ri
            keep = diff > 0.0
            qd = jnp.exp((chunk - r128) * lgk)
            kd = jnp.exp(r128 * lgk)
        mask_ref[h] = jnp.where(keep, jnp.exp(lgc * jnp.maximum(diff, 0.0)), 0.0) * scale
        qd_ref[h] = qd
        kd_ref[h] = kd * scale
        cd_ref[h] = jnp.broadcast_to(jnp.exp(float(chunk) * lg[:, :DV]), (8, DV))


def _retention_chunk(q, k, v, h, S_ref, mask_ref, qd_ref, kd_ref, cd_ref):
    qb = q.astype(BF16)
    kb = k.astype(BF16)
    vb = v.astype(BF16)
    sc = lax.dot_general(qb, kb, (((1,), (1,)), ((), ())), preferred_element_type=F32) * mask_ref[h]
    S = S_ref[h]
    y = _dot(sc.astype(BF16), vb) + _dot((q * qd_ref[h]).astype(BF16), S.astype(BF16))
    upd = lax.dot_general((k * kd_ref[h]).astype(BF16), vb, (((0,), (0,)), ((), ())),
                          preferred_element_type=F32)
    S_ref[h] = S * cd_ref[h][0:1, :] + upd
    return y


def _ret_bwd_kernel(x_ref, g_ref, w_ref, cos_ref, sin_ref, dl_ref, o_ref,
                    S_ref, mask_ref, qd_ref, kd_ref, cd_ref):
    b = pl.program_id(0)
    i = pl.program_id(1)
    chunk = x_ref.shape[1]

    @pl.when((b == 0) & (i == 0))
    def _():
        _init_decay_tables(dl_ref, mask_ref, qd_ref, kd_ref, cd_ref, chunk, fwd=False)

    @pl.when(i == 0)
    def _():
        S_ref[...] = jnp.zeros_like(S_ref)

    h = _rms(x_ref[0], g_ref[...]).astype(BF16)
    c = cos_ref[...]
    s = sin_ref[...]
    qk = _dot(h, w_ref[:, 0:2 * QK_WIDTH])
    v = _dot(h, w_ref[:, 2 * QK_WIDTH:])
    for hd in range(HEADS):
        qh = _rot(qk[:, hd * DK:(hd + 1) * DK], c, s)
        kh = _rot(qk[:, QK_WIDTH + hd * DK:QK_WIDTH + (hd + 1) * DK], c, s)
        vh = v[:, hd * DV:(hd + 1) * DV]
        o_ref[0, :, hd * DV:(hd + 1) * DV] = _retention_chunk(
            qh, kh, vh, hd, S_ref, mask_ref, qd_ref, kd_ref, cd_ref)


def _mixer_kernel(x_ref, xp_ref, xn_ref, bwd_ref, cos_ref, sin_ref, dl_ref, g_ref, win_ref,
                  pw_ref, ps_ref, wpo_ref, wro_ref, wo_ref, o_ref,
                  S_ref, mask_ref, qd_ref, kd_ref, cd_ref, hext_ref, xa_ref, *, seq_len):
    b = pl.program_id(0)
    i = pl.program_id(1)
    nt = pl.num_programs(1)
    T = x_ref.shape[1]

    @pl.when((b == 0) & (i == 0))
    def _():
        _init_decay_tables(dl_ref, mask_ref, qd_ref, kd_ref, cd_ref, T, fwd=True)

    @pl.when(i == 0)
    def _():
        S_ref[...] = jnp.zeros_like(S_ref)

    g = g_ref[...]
    x = x_ref[0]
    xp = jnp.where(i > 0, xp_ref[0], 0.0)
    xn = jnp.where(i < nt - 1, xn_ref[0], 0.0)
    hext_ref[0:HALO, :] = _rms(xp, g).astype(BF16)
    hext_ref[HALO:HALO + T, :] = _rms(x, g).astype(BF16)
    hext_ref[HALO + T:, :] = _rms(xn, g).astype(BF16)
    xa_ref[...] = _dot(hext_ref[...], win_ref[:, 0:POOL_WIDTH])
    h = hext_ref[HALO:HALO + T, :]

    tglob = i * T + lax.broadcasted_iota(jnp.int32, (T, POOL_GROUP_DIM), 0)
    pooled = []
    for gi, w in enumerate(POOL_WINDOWS):
        cols = slice(gi * POOL_GROUP_DIM, (gi + 1) * POOL_GROUP_DIM)
        acc = xa_ref[HALO - w // 2:HALO - w // 2 + T, cols]
        for kk in range(-w // 2 + 1, w // 2):
            acc = acc + xa_ref[HALO + kk:HALO + kk + T, cols]
        lo = jnp.maximum(tglob - w // 2, 0)
        hi = jnp.minimum(tglob + w // 2, seq_len)
        mean = acc / (hi - lo).astype(F32)
        d = mean - xa_ref[HALO:HALO + T, cols]
        pooled.append(_dot(d.astype(BF16), pw_ref[gi]))
    pooled = jnp.concatenate(pooled, axis=-1) * ps_ref[...]
    a = _dot(pooled.astype(BF16), wpo_ref[...])

    c = cos_ref[...]
    s = sin_ref[...]
    qk = _dot(h, win_ref[:, OFF_Q:OFF_V])
    v = _dot(h, win_ref[:, OFF_V:OFF_G])
    xg = _dot(h, win_ref[:, OFF_G:OFF_GATE])
    rin = []
    for hd in range(HEADS):
        qh = _rot(qk[:, hd * DK:(hd + 1) * DK], c, s)
        kh = _rot(qk[:, QK_WIDTH + hd * DK:QK_WIDTH + (hd + 1) * DK], c, s)
        vh = v[:, hd * DV:(hd + 1) * DV]
        y = _retention_chunk(qh, kh, vh, hd, S_ref, mask_ref, qd_ref, kd_ref, cd_ref)
        y = y + bwd_ref[0, :, hd * DV:(hd + 1) * DV]
        yc = y - jnp.mean(y, axis=-1, keepdims=True)
        yn = yc * lax.rsqrt(jnp.mean(yc * yc, axis=-1, keepdims=True) + GN_EPS)
        gh = xg[:, hd * DV:(hd + 1) * DV]
        rin.append((yn * (gh * jax.nn.sigmoid(gh))).astype(BF16))
    r = _dot(jnp.concatenate(rin, axis=-1), wro_ref[...])

    gates = jax.nn.sigmoid(_dot(h, win_ref[:, OFF_GATE:]))
    merged = gates[:, :D_MODEL] * a + gates[:, D_MODEL:] * r
    o_ref[0] = x + _dot(merged.astype(BF16), wo_ref[...])


def _ffn_kernel(x_ref, p_ref, gf_ref, wfi_ref, wfo_ref, gp_ref, wpg_ref, wpp_ref, gfin_ref, o_ref,
                *, final_norm):
    x = x_ref[0]
    h2 = _rms(x, gf_ref[...]).astype(BF16)
    gt = _dot(h2, wfi_ref[:, :D_FF])
    up = _dot(h2, wfi_ref[:, D_FF:])
    x = x + _dot((gt * jax.nn.sigmoid(gt) * up).astype(BF16), wfo_ref[...])
    pg = jax.nn.sigmoid(_dot(_rms(x, gp_ref[...]).astype(BF16), wpg_ref[...]))
    x = x + _dot(p_ref[0].astype(BF16), wpp_ref[...]) * pg
    if final_norm:
        x = _rms(x, gfin_ref[...])
    o_ref[0] = x


def _const_spec(shape):
    nd = len(shape)
    return pl.BlockSpec(shape, lambda b, i: (0,) * nd, pipeline_mode=pl.Buffered(1))


def _retention_scratch(chunk):
    return [
        pltpu.VMEM((HEADS, DK, DV), F32),
        pltpu.VMEM((HEADS, chunk, chunk), F32),
        pltpu.VMEM((HEADS, chunk, DK), F32),
        pltpu.VMEM((HEADS, chunk, DK), F32),
        pltpu.VMEM((HEADS, 8, DV), F32),
    ]


def _ret_bwd(x, g_mix, w_qkv, cos_t, sin_t, dl):
    B, L, D = x.shape
    T = T_MIX
    nt = L // T
    return pl.pallas_call(
        _ret_bwd_kernel,
        grid=(B, nt),
        in_specs=[
            pl.BlockSpec((1, T, D), lambda b, i: (b, nt - 1 - i, 0)),
            _const_spec((1, D)),
            _const_spec(w_qkv.shape),
            pl.BlockSpec((T, DK), lambda b, i: (nt - 1 - i, 0)),
            pl.BlockSpec((T, DK), lambda b, i: (nt - 1 - i, 0)),
            _const_spec(dl.shape),
        ],
        out_specs=pl.BlockSpec((1, T, V_WIDTH), lambda b, i: (b, nt - 1 - i, 0)),
        out_shape=jax.ShapeDtypeStruct((B, L, V_WIDTH), F32),
        scratch_shapes=_retention_scratch(T),
        compiler_params=pltpu.CompilerParams(
            dimension_semantics=("arbitrary", "arbitrary"), vmem_limit_bytes=VMEM_LIMIT),
        name="ret_bwd",
    )(x, g_mix, w_qkv, cos_t, sin_t, dl)


def _mixer(x, bwd, g_mix, w_in, pool_w, pool_scale, w_pool_out, w_ret_out, w_o, cos_t, sin_t, dl):
    B, L, D = x.shape
    T = T_MIX
    nt = L // T
    hb = T // HALO
    nhb = L // HALO
    return pl.pallas_call(
        functools.partial(_mixer_kernel, seq_len=L),
        grid=(B, nt),
        in_specs=[
            pl.BlockSpec((1, T, D), lambda b, i: (b, i, 0)),
            pl.BlockSpec((1, HALO, D), lambda b, i: (b, jnp.maximum(i * hb - 1, 0), 0)),
            pl.BlockSpec((1, HALO, D), lambda b, i: (b, jnp.minimum((i + 1) * hb, nhb - 1), 0)),
            pl.BlockSpec((1, T, V_WIDTH), lambda b, i: (b, i, 0)),
            pl.BlockSpec((T, DK), lambda b, i: (i, 0)),
            pl.BlockSpec((T, DK), lambda b, i: (i, 0)),
            _const_spec(dl.shape),
            _const_spec((1, D)),
            _const_spec(w_in.shape),
            _const_spec(pool_w.shape),
            _const_spec((1, POOL_WIDTH)),
            _const_spec(w_pool_out.shape),
            _const_spec(w_ret_out.shape),
            _const_spec(w_o.shape),
        ],
        out_specs=pl.BlockSpec((1, T, D), lambda b, i: (b, i, 0)),
        out_shape=jax.ShapeDtypeStruct((B, L, D), F32),
        scratch_shapes=_retention_scratch(T) + [
            pltpu.VMEM((T + 2 * HALO, D), BF16),
            pltpu.VMEM((T + 2 * HALO, POOL_WIDTH), F32),
        ],
        compiler_params=pltpu.CompilerParams(
            dimension_semantics=("arbitrary", "arbitrary"), vmem_limit_bytes=VMEM_LIMIT),
        name="mixer",
    )(x, x, x, bwd, cos_t, sin_t, dl, g_mix, w_in, pool_w, pool_scale, w_pool_out, w_ret_out, w_o)


def _ffn(x, p, g_ffn, w_ffn_in, w_ffn_out, g_ple, w_ple_gate, w_ple_proj, g_final, final_norm):
    B, L, D = x.shape
    T = T_FFN
    return pl.pallas_call(
        functools.partial(_ffn_kernel, final_norm=final_norm),
        grid=(B, L // T),
        in_specs=[
            pl.BlockSpec((1, T, D), lambda b, i: (b, i, 0)),
            pl.BlockSpec((1, T, PLE_DIM), lambda b, i: (b, i, 0)),
            _const_spec((1, D)),
            _const_spec(w_ffn_in.shape),
            _const_spec(w_ffn_out.shape),
            _const_spec((1, D)),
            _const_spec(w_ple_gate.shape),
            _const_spec(w_ple_proj.shape),
            _const_spec((1, D)),
        ],
        out_specs=pl.BlockSpec((1, T, D), lambda b, i: (b, i, 0)),
        out_shape=jax.ShapeDtypeStruct((B, L, D), F32),
        compiler_params=pltpu.CompilerParams(
            dimension_semantics=("parallel", "parallel"), vmem_limit_bytes=VMEM_LIMIT),
        name="ffn",
    )(x, p, g_ffn, w_ffn_in, w_ffn_out, g_ple, w_ple_gate, w_ple_proj, g_final)


def _rotary_tables(L):
    half = DK // 2
    inv = 1.0 / (ROPE_BASE ** (jnp.arange(half, dtype=F32) / half))
    ang = jnp.arange(L).astype(F32)[:, None] * inv[None, :]
    cos = jnp.cos(ang)
    sin = jnp.sin(ang)
    return jnp.concatenate([cos, cos], axis=-1), jnp.concatenate([-sin, sin], axis=-1)


def kernel(x_prompt, x_sample, p_prompt, p_sample, g_mix, w_in, pool_w, pool_scale, decay_logit,
           w_pool_out, w_ret_out, w_o, g_ffn, w_ffn_in, w_ffn_out, g_ple, w_ple_gate, w_ple_proj,
           g_final):
    depth = w_in.shape[0]
    L = x_prompt.shape[1]
    cos_t, sin_t = _rotary_tables(L)
    w_in_b = w_in.astype(BF16)
    w_qkv_b = w_in_b[:, :, OFF_Q:OFF_G]
    pool_w_b = pool_w.astype(BF16)
    w_pool_out_b = w_pool_out.astype(BF16)
    w_ret_out_b = w_ret_out.astype(BF16)
    w_o_b = w_o.astype(BF16)
    w_ffn_in_b = w_ffn_in.astype(BF16)
    w_ffn_out_b = w_ffn_out.astype(BF16)
    w_ple_gate_b = w_ple_gate.astype(BF16)
    w_ple_proj_b = w_ple_proj.astype(BF16)
    dl = jnp.broadcast_to(decay_logit.astype(F32)[..., None, None], decay_logit.shape + (1, DV))
    row = lambda v: v.reshape(1, -1)

    def trunk(x, p):
        for l in range(depth):
            bwd = _ret_bwd(x, row(g_mix[l]), w_qkv_b[l], cos_t, sin_t, dl[l, 1])
            x = _mixer(x, bwd, row(g_mix[l]), w_in_b[l], pool_w_b[l], row(pool_scale[l]),
                       w_pool_out_b[l], w_ret_out_b[l], w_o_b[l], cos_t, sin_t, dl[l, 0])
            x = _ffn(x, p[l], row(g_ffn[l]), w_ffn_in_b[l], w_ffn_out_b[l], row(g_ple[l]),
                     w_ple_gate_b[l], w_ple_proj_b[l], row(g_final), l == depth - 1)
        return x

    return trunk(x_prompt, p_prompt), trunk(x_sample, p_sample)
```

```python
import functools

import jax
import jax.numpy as jnp
from jax import lax
from jax.experimental import pallas as pl
from jax.experimental.pallas import tpu as pltpu

F32 = jnp.float32
BF16 = jnp.bfloat16

D_MODEL = 1024
PLE_DIM = 256
POOL_GROUP_DIM = 128
POOL_WIDTH = 512
POOL_WINDOWS = (2, 4, 8, 16)
HEADS = 4
DK = 128
DV = 256
QK_WIDTH = HEADS * DK
V_WIDTH = HEADS * DV
D_FF = 2816
ROPE_BASE = 10000.0
RMS_EPS = 1e-6
GN_EPS = 1e-5

OFF_Q = POOL_WIDTH
OFF_K = OFF_Q + QK_WIDTH
OFF_V = OFF_K + QK_WIDTH
OFF_G = OFF_V + V_WIDTH
OFF_GATE = OFF_G + V_WIDTH

HALO = 16
CHUNK = 256
T_FFN = 512
VMEM_LIMIT = 56 * 1024 * 1024
FWD, BWD = 0, 1


def _rms(x, g):
    return x * lax.rsqrt(jnp.mean(x * x, axis=-1, keepdims=True) + RMS_EPS) * g


def _rot(x, c, s):
    return x * c + pltpu.roll(x, DK // 2, 1) * s


def _dot(a, b):
    return jnp.dot(a, b, preferred_element_type=F32)


def _dot_tn(a, b):
    return lax.dot_general(a, b, (((0,), (0,)), ((), ())), preferred_element_type=F32)


def _dot_nt(a, b):
    return lax.dot_general(a, b, (((1,), (1,)), ((), ())), preferred_element_type=F32)


def _log_decay(dl_ref, direction, h):
    x = dl_ref[direction, h]
    return jnp.minimum(x, 0.0) - jnp.log1p(jnp.exp(-jnp.abs(x)))


def _kv_kernel(x_ref, g_ref, w_ref, cos_ref, sin_ref, dl_ref,
               h_ref, k_ref, kf_ref, v_ref, sb_ref,
               S_ref, kdf_ref, kdb_ref, cdb_ref):
    b = pl.program_id(0)
    i = pl.program_id(1)

    @pl.when((b == 0) & (i == 0))
    def _():
        r = lax.broadcasted_iota(jnp.int32, (CHUNK, DK), 0).astype(F32)
        for hd in range(HEADS):
            lgf = _log_decay(dl_ref, FWD, hd)
            lgb = _log_decay(dl_ref, BWD, hd)
            kdf_ref[hd] = jnp.exp((CHUNK - 1.0 - r) * lgf[:, :DK]) * DK ** -0.5
            kdb_ref[hd] = jnp.exp(r * lgb[:, :DK]) * DK ** -0.5
            cdb_ref[hd] = jnp.broadcast_to(jnp.exp(float(CHUNK) * lgb), (8, DV))

    @pl.when(i == 0)
    def _():
        S_ref[...] = jnp.zeros_like(S_ref)

    h = _rms(x_ref[0], g_ref[...]).astype(BF16)
    h_ref[0] = h
    c = cos_ref[...]
    s = sin_ref[...]
    k = _dot(h, w_ref[:, OFF_K:OFF_V])
    v = _dot(h, w_ref[:, OFF_V:OFF_G]).astype(BF16)
    v_ref[0] = v
    for hd in range(HEADS):
        kh = _rot(k[:, hd * DK:(hd + 1) * DK], c, s)
        k_ref[0, :, hd * DK:(hd + 1) * DK] = kh.astype(BF16)
        kf_ref[0, :, hd * DK:(hd + 1) * DK] = (kh * kdf_ref[hd]).astype(BF16)
        S = S_ref[hd]
        sb_ref[0, 0, hd] = S.astype(BF16)
        upd = _dot_tn((kh * kdb_ref[hd]).astype(BF16), v[:, hd * DV:(hd + 1) * DV])
        S_ref[hd] = S * cdb_ref[hd][0:1, :] + upd


def _mixer_kernel(x_ref, h_ref, hp_ref, hn_ref, k_ref, kf_ref, v_ref, sb_ref, cos_ref, sin_ref, dl_ref,
                  win_ref, pw_ref, ps_ref, wpo_ref, wro_ref, wo_ref, o_ref,
                  S_ref, mask_ref, qd_ref, cdf_ref, hext_ref, xa_ref, *, seq_len):
    b = pl.program_id(0)
    i = pl.program_id(1)
    nt = pl.num_programs(1)
    T = CHUNK

    @pl.when((b == 0) & (i == 0))
    def _():
        ri = lax.broadcasted_iota(jnp.int32, (T, T), 0).astype(F32)
        ci = lax.broadcasted_iota(jnp.int32, (T, T), 1).astype(F32)
        r = lax.broadcasted_iota(jnp.int32, (T, DK), 0).astype(F32)
        for hd in range(HEADS):
            lgf = _log_decay(dl_ref, FWD, hd)
            lgb = _log_decay(dl_ref, BWD, hd)
            dfw = ri - ci
            m = jnp.where(dfw >= 0.0,
                          jnp.exp(lgf[:, :T] * jnp.maximum(dfw, 0.0)),
                          jnp.exp(lgb[:, :T] * jnp.maximum(-dfw, 0.0)))
            mask_ref[hd] = m * DK ** -0.5
            qd_ref[hd, :, 0:DK] = jnp.exp((r + 1.0) * lgf[:, :DK])
            qd_ref[hd, :, DK:2 * DK] = jnp.exp((T - r) * lgb[:, :DK])
            cdf_ref[hd] = jnp.broadcast_to(jnp.exp(float(T) * lgf), (8, DV))

    @pl.when(i == 0)
    def _():
        S_ref[...] = jnp.zeros_like(S_ref)

    h = h_ref[0]
    hext_ref[0:HALO, :] = jnp.where(i > 0, hp_ref[0], jnp.zeros_like(hp_ref[0]))
    hext_ref[HALO:HALO + T, :] = h
    hext_ref[HALO + T:, :] = jnp.where(i < nt - 1, hn_ref[0], jnp.zeros_like(hn_ref[0]))
    xa_ref[...] = _dot(hext_ref[...], win_ref[:, 0:POOL_WIDTH])

    tglob = i * T + lax.broadcasted_iota(jnp.int32, (T, POOL_GROUP_DIM), 0)
    pooled = []
    for gi, w in enumerate(POOL_WINDOWS):
        cols = slice(gi * POOL_GROUP_DIM, (gi + 1) * POOL_GROUP_DIM)
        acc = xa_ref[HALO - w // 2:HALO - w // 2 + T, cols]
        for kk in range(-w // 2 + 1, w // 2):
            acc = acc + xa_ref[HALO + kk:HALO + kk + T, cols]
        lo = jnp.maximum(tglob - w // 2, 0)
        hi = jnp.minimum(tglob + w // 2, seq_len)
        mean = acc / (hi - lo).astype(F32)
        d = mean - xa_ref[HALO:HALO + T, cols]
        pooled.append(_dot(d.astype(BF16), pw_ref[gi]))
    pooled = jnp.concatenate(pooled, axis=-1) * ps_ref[...]
    a = _dot(pooled.astype(BF16), wpo_ref[...])

    c = cos_ref[...]
    s = sin_ref[...]
    q = _dot(h, win_ref[:, OFF_Q:OFF_K])
    xg = _dot(h, win_ref[:, OFF_G:OFF_GATE])
    rin = []
    for hd in range(HEADS):
        qh = _rot(q[:, hd * DK:(hd + 1) * DK], c, s)
        kb = k_ref[0, :, hd * DK:(hd + 1) * DK]
        vb = v_ref[0, :, hd * DV:(hd + 1) * DV]
        S = S_ref[hd]
        sc = _dot_nt(qh.astype(BF16), kb) * mask_ref[hd]
        qdec = (jnp.concatenate([qh, qh], axis=-1) * qd_ref[hd]).astype(BF16)
        states = jnp.concatenate([S.astype(BF16), sb_ref[0, 0, hd]], axis=0)
        y = _dot(sc.astype(BF16), vb) + _dot(qdec, states)
        S_ref[hd] = S * cdf_ref[hd][0:1, :] + _dot_tn(kf_ref[0, :, hd * DK:(hd + 1) * DK], vb)
        yc = y - jnp.mean(y, axis=-1, keepdims=True)
        yn = yc * lax.rsqrt(jnp.mean(yc * yc, axis=-1, keepdims=True) + GN_EPS)
        gh = xg[:, hd * DV:(hd + 1) * DV]
        rin.append((yn * (gh * jax.nn.sigmoid(gh))).astype(BF16))
    r = _dot(jnp.concatenate(rin, axis=-1), wro_ref[...])

    gates = jax.nn.sigmoid(_dot(h, win_ref[:, OFF_GATE:]))
    merged = gates[:, :D_MODEL] * a + gates[:, D_MODEL:] * r
    o_ref[0] = x_ref[0] + _dot(merged.astype(BF16), wo_ref[...])


def _ffn_kernel(x_ref, p_ref, gf_ref, wfi_ref, wfo_ref, gp_ref, wpg_ref, wpp_ref, gfin_ref, o_ref,
                *, final_norm):
    x = x_ref[0]
    h2 = _rms(x, gf_ref[...]).astype(BF16)
    gt = _dot(h2, wfi_ref[:, :D_FF])
    up = _dot(h2, wfi_ref[:, D_FF:])
    x = x + _dot((gt * jax.nn.sigmoid(gt) * up).astype(BF16), wfo_ref[...])
    pg = jax.nn.sigmoid(_dot(_rms(x, gp_ref[...]).astype(BF16), wpg_ref[...]))
    x = x + _dot(p_ref[0].astype(BF16), wpp_ref[...]) * pg
    if final_norm:
        x = _rms(x, gfin_ref[...])
    o_ref[0] = x


def _layer_spec(arr, layer):
    nd = arr.ndim - 1
    return pl.BlockSpec((None,) + arr.shape[1:], lambda b, i: (layer,) + (0,) * nd,
                        pipeline_mode=pl.Buffered(1))


def _const_spec(arr):
    nd = arr.ndim
    return pl.BlockSpec(arr.shape, lambda b, i: (0,) * nd, pipeline_mode=pl.Buffered(1))


def _params(semantics):
    return pltpu.CompilerParams(dimension_semantics=semantics, vmem_limit_bytes=VMEM_LIMIT)


def _kv(x, layer, g_mix, w_in, cos_t, sin_t, dl):
    B, L, D = x.shape
    T = CHUNK
    nt = L // T
    rev = lambda b, i: (b, nt - 1 - i, 0)
    return pl.pallas_call(
        _kv_kernel,
        grid=(B, nt),
        in_specs=[
            pl.BlockSpec((1, T, D), rev),
            _layer_spec(g_mix, layer),
            _layer_spec(w_in, layer),
            pl.BlockSpec((T, DK), lambda b, i: (nt - 1 - i, 0)),
            pl.BlockSpec((T, DK), lambda b, i: (nt - 1 - i, 0)),
            _layer_spec(dl, layer),
        ],
        out_specs=[
            pl.BlockSpec((1, T, D), rev),
            pl.BlockSpec((1, T, QK_WIDTH), rev),
            pl.BlockSpec((1, T, QK_WIDTH), rev),
            pl.BlockSpec((1, T, V_WIDTH), rev),
            pl.BlockSpec((1, 1, HEADS, DK, DV), lambda b, i: (b, nt - 1 - i, 0, 0, 0)),
        ],
        out_shape=[
            jax.ShapeDtypeStruct((B, L, D), BF16),
            jax.ShapeDtypeStruct((B, L, QK_WIDTH), BF16),
            jax.ShapeDtypeStruct((B, L, QK_WIDTH), BF16),
            jax.ShapeDtypeStruct((B, L, V_WIDTH), BF16),
            jax.ShapeDtypeStruct((B, nt, HEADS, DK, DV), BF16),
        ],
        scratch_shapes=[
            pltpu.VMEM((HEADS, DK, DV), F32),
            pltpu.VMEM((HEADS, T, DK), F32),
            pltpu.VMEM((HEADS, T, DK), F32),
            pltpu.VMEM((HEADS, 8, DV), F32),
        ],
        compiler_params=_params(("arbitrary", "arbitrary")),
        name="kv",
    )(x, g_mix, w_in, cos_t, sin_t, dl)


def _mixer(x, h, k, kf, v, sb, layer, w_in, pool_w, pool_scale, w_pool_out, w_ret_out, w_o,
           cos_t, sin_t, dl):
    B, L, D = x.shape
    T = CHUNK
    nt = L // T
    hb = T // HALO
    nhb = L // HALO
    tile = lambda b, i: (b, i, 0)
    return pl.pallas_call(
        functools.partial(_mixer_kernel, seq_len=L),
        grid=(B, nt),
        in_specs=[
            pl.BlockSpec((1, T, D), tile),
            pl.BlockSpec((1, T, D), tile),
            pl.BlockSpec((1, HALO, D), lambda b, i: (b, jnp.maximum(i * hb - 1, 0), 0)),
            pl.BlockSpec((1, HALO, D), lambda b, i: (b, jnp.minimum((i + 1) * hb, nhb - 1), 0)),
            pl.BlockSpec((1, T, QK_WIDTH), tile),
            pl.BlockSpec((1, T, QK_WIDTH), tile),
            pl.BlockSpec((1, T, V_WIDTH), tile),
            pl.BlockSpec((1, 1, HEADS, DK, DV), lambda b, i: (b, i, 0, 0, 0)),
            pl.BlockSpec((T, DK), lambda b, i: (i, 0)),
            pl.BlockSpec((T, DK), lambda b, i: (i, 0)),
            _layer_spec(dl, layer),
            _layer_spec(w_in, layer),
            _layer_spec(pool_w, layer),
            _layer_spec(pool_scale, layer),
            _layer_spec(w_pool_out, layer),
            _layer_spec(w_ret_out, layer),
            _layer_spec(w_o, layer),
        ],
        out_specs=pl.BlockSpec((1, T, D), tile),
        out_shape=jax.ShapeDtypeStruct((B, L, D), F32),
        scratch_shapes=[
            pltpu.VMEM((HEADS, DK, DV), F32),
            pltpu.VMEM((HEADS, T, T), F32),
            pltpu.VMEM((HEADS, T, 2 * DK), F32),
            pltpu.VMEM((HEADS, 8, DV), F32),
            pltpu.VMEM((T + 2 * HALO, D), BF16),
            pltpu.VMEM((T + 2 * HALO, POOL_WIDTH), F32),
        ],
        compiler_params=_params(("arbitrary", "arbitrary")),
        name="mixer",
    )(x, h, h, h, k, kf, v, sb, cos_t, sin_t, dl, w_in, pool_w, pool_scale, w_pool_out, w_ret_out, w_o)


def _ffn(x, p, layer, g_ffn, w_ffn_in, w_ffn_out, g_ple, w_ple_gate, w_ple_proj, g_final, final_norm):
    B, L, D = x.shape
    T = T_FFN
    tile = lambda b, i: (b, i, 0)
    return pl.pallas_call(
        functools.partial(_ffn_kernel, final_norm=final_norm),
        grid=(B, L // T),
        in_specs=[
            pl.BlockSpec((1, T, D), tile),
            pl.BlockSpec((None, 1, T, PLE_DIM), lambda b, i: (layer, b, i, 0)),
            _layer_spec(g_ffn, layer),
            _layer_spec(w_ffn_in, layer),
            _layer_spec(w_ffn_out, layer),
            _layer_spec(g_ple, layer),
            _layer_spec(w_ple_gate, layer),
            _layer_spec(w_ple_proj, layer),
            _const_spec(g_final),
        ],
        out_specs=pl.BlockSpec((1, T, D), tile),
        out_shape=jax.ShapeDtypeStruct((B, L, D), F32),
        compiler_params=_params(("parallel", "parallel")),
        name="ffn",
    )(x, p, g_ffn, w_ffn_in, w_ffn_out, g_ple, w_ple_gate, w_ple_proj, g_final)


def _rotary_tables(L):
    half = DK // 2
    inv = 1.0 / (ROPE_BASE ** (jnp.arange(half, dtype=F32) / half))
    ang = jnp.arange(L).astype(F32)[:, None] * inv[None, :]
    cos = jnp.cos(ang)
    sin = jnp.sin(ang)
    return jnp.concatenate([cos, cos], axis=-1), jnp.concatenate([-sin, sin], axis=-1)


def kernel(x_prompt, x_sample, p_prompt, p_sample, g_mix, w_in, pool_w, pool_scale, decay_logit,
           w_pool_out, w_ret_out, w_o, g_ffn, w_ffn_in, w_ffn_out, g_ple, w_ple_gate, w_ple_proj,
           g_final):
    depth = w_in.shape[0]
    L = x_prompt.shape[1]
    cos_t, sin_t = _rotary_tables(L)
    w_in_b = w_in.astype(BF16)
    pool_w_b = pool_w.astype(BF16)
    w_pool_out_b = w_pool_out.astype(BF16)
    w_ret_out_b = w_ret_out.astype(BF16)
    w_o_b = w_o.astype(BF16)
    w_ffn_in_b = w_ffn_in.astype(BF16)
    w_ffn_out_b = w_ffn_out.astype(BF16)
    w_ple_gate_b = w_ple_gate.astype(BF16)
    w_ple_proj_b = w_ple_proj.astype(BF16)
    dl = jnp.broadcast_to(decay_logit.astype(F32)[..., None, None], decay_logit.shape + (1, DV))
    rows = lambda v: v.reshape(v.shape[0], 1, v.shape[1])
    g_mix_r, g_ffn_r, g_ple_r, pool_scale_r = rows(g_mix), rows(g_ffn), rows(g_ple), rows(pool_scale)
    g_final_r = g_final.reshape(1, -1)

    def trunk(x, p):
        for l in range(depth):
            h, k, kf, v, sb = _kv(x, l, g_mix_r, w_in_b, cos_t, sin_t, dl)
            x = _mixer(x, h, k, kf, v, sb, l, w_in_b, pool_w_b, pool_scale_r, w_pool_out_b,
                       w_ret_out_b, w_o_b, cos_t, sin_t, dl)
            x = _ffn(x, p, l, g_ffn_r, w_ffn_in_b, w_ffn_out_b, g_ple_r, w_ple_gate_b,
                     w_ple_proj_b, g_final_r, l == depth - 1)
        return x

    return trunk(x_prompt, p_prompt), trunk(x_sample, p_sample)
```

```python
import functools

import jax
import jax.numpy as jnp
from jax import lax
from jax.experimental import pallas as pl
from jax.experimental.pallas import tpu as pltpu

F32 = jnp.float32
BF16 = jnp.bfloat16

D_MODEL = 1024
PLE_DIM = 256
POOL_GROUP_DIM = 128
POOL_WIDTH = 512
POOL_WINDOWS = (2, 4, 8, 16)
HEADS = 4
DK = 128
DV = 256
QK_WIDTH = HEADS * DK
V_WIDTH = HEADS * DV
D_FF = 2816
ROPE_BASE = 10000.0
RMS_EPS = 1e-6
GN_EPS = 1e-5

OFF_Q = POOL_WIDTH
OFF_K = OFF_Q + QK_WIDTH
OFF_V = OFF_K + QK_WIDTH
OFF_G = OFF_V + V_WIDTH
OFF_GATE = OFF_G + V_WIDTH

HALO = 16
CHUNK = 256
T_KV = 512
T_FFN = 512
FFN_GROUPS = 2
VMEM_LIMIT = 56 * 1024 * 1024
FWD, BWD = 0, 1


def _rms(x, g):
    return x * lax.rsqrt(jnp.mean(x * x, axis=-1, keepdims=True) + RMS_EPS) * g


def _rot(x, c, s):
    return x * c + pltpu.roll(x, DK // 2, 1) * s


def _dot(a, b):
    return jnp.dot(a, b, preferred_element_type=F32)


def _dot_tn(a, b):
    return lax.dot_general(a, b, (((0,), (0,)), ((), ())), preferred_element_type=F32)


def _dot_nt(a, b):
    return lax.dot_general(a, b, (((1,), (1,)), ((), ())), preferred_element_type=F32)


def _log_decay(dl_ref, direction, h):
    x = dl_ref[direction, h]
    return jnp.minimum(x, 0.0) - jnp.log1p(jnp.exp(-jnp.abs(x)))


def _kv_kernel(x_ref, g_ref, w_ref, cos_ref, sin_ref, dl_ref,
               h_ref, k_ref, kf_ref, v_ref, sb_ref,
               S_ref, kdf_ref, kdb_ref, cdb_ref):
    b = pl.program_id(0)
    i = pl.program_id(1)

    @pl.when((b == 0) & (i == 0))
    def _():
        r = lax.broadcasted_iota(jnp.int32, (CHUNK, DK), 0).astype(F32)
        for hd in range(HEADS):
            lgf = _log_decay(dl_ref, FWD, hd)
            lgb = _log_decay(dl_ref, BWD, hd)
            kdf_ref[hd] = jnp.exp((CHUNK - 1.0 - r) * lgf[:, :DK]) * DK ** -0.5
            kdb_ref[hd] = jnp.exp(r * lgb[:, :DK]) * DK ** -0.5
            cdb_ref[hd] = jnp.broadcast_to(jnp.exp(float(CHUNK) * lgb), (8, DV))

    @pl.when(i == 0)
    def _():
        S_ref[...] = jnp.zeros_like(S_ref)

    chunks = [slice(ci * CHUNK, (ci + 1) * CHUNK) for ci in reversed(range(x_ref.shape[1] // CHUNK))]
    ks, vs = [], []
    for rows in chunks:
        h = _rms(x_ref[0, rows, :], g_ref[...]).astype(BF16)
        h_ref[0, rows, :] = h
        ks.append(_dot(h, w_ref[:, OFF_K:OFF_V]))
        v = _dot(h, w_ref[:, OFF_V:OFF_G]).astype(BF16)
        v_ref[0, rows, :] = v
        vs.append(v)
    for rows, k, v in zip(chunks, ks, vs):
        c = cos_ref[rows, :]
        s = sin_ref[rows, :]
        ci = rows.start // CHUNK
        for hd in range(HEADS):
            kh = _rot(k[:, hd * DK:(hd + 1) * DK], c, s)
            k_ref[0, rows, hd * DK:(hd + 1) * DK] = kh.astype(BF16)
            kf_ref[0, rows, hd * DK:(hd + 1) * DK] = (kh * kdf_ref[hd]).astype(BF16)
            S = S_ref[hd]
            sb_ref[0, ci, hd] = S.astype(BF16)
            upd = _dot_tn((kh * kdb_ref[hd]).astype(BF16), v[:, hd * DV:(hd + 1) * DV])
            S_ref[hd] = S * cdb_ref[hd][0:1, :] + upd


def _mixer_kernel(x_ref, h_ref, hp_ref, hn_ref, k_ref, kf_ref, v_ref, sb_ref, cos_ref, sin_ref, dl_ref,
                  win_ref, pw_ref, ps_ref, wpo_ref, wro_ref, wo_ref, o_ref,
                  S_ref, mask_ref, qd_ref, cdf_ref, hext_ref, xa_ref, *, seq_len):
    b = pl.program_id(0)
    i = pl.program_id(1)
    nt = pl.num_programs(1)
    T = CHUNK

    @pl.when((b == 0) & (i == 0))
    def _():
        ri = lax.broadcasted_iota(jnp.int32, (T, T), 0).astype(F32)
        ci = lax.broadcasted_iota(jnp.int32, (T, T), 1).astype(F32)
        r = lax.broadcasted_iota(jnp.int32, (T, DK), 0).astype(F32)
        for hd in range(HEADS):
            lgf = _log_decay(dl_ref, FWD, hd)
            lgb = _log_decay(dl_ref, BWD, hd)
            dfw = ri - ci
            m = jnp.where(dfw >= 0.0,
                          jnp.exp(lgf[:, :T] * jnp.maximum(dfw, 0.0)),
                          jnp.exp(lgb[:, :T] * jnp.maximum(-dfw, 0.0)))
            mask_ref[hd] = m * DK ** -0.5
            qd_ref[hd, :, 0:DK] = jnp.exp((r + 1.0) * lgf[:, :DK])
            qd_ref[hd, :, DK:2 * DK] = jnp.exp((T - r) * lgb[:, :DK])
            cdf_ref[hd] = jnp.broadcast_to(jnp.exp(float(T) * lgf), (8, DV))

    @pl.when(i == 0)
    def _():
        S_ref[...] = jnp.zeros_like(S_ref)

    h = h_ref[0]
    c = cos_ref[...]
    s = sin_ref[...]
    q = _dot(h, win_ref[:, OFF_Q:OFF_K])

    hext_ref[0:HALO, :] = jnp.where(i > 0, hp_ref[0], jnp.zeros_like(hp_ref[0]))
    hext_ref[HALO:HALO + T, :] = h
    hext_ref[HALO + T:, :] = jnp.where(i < nt - 1, hn_ref[0], jnp.zeros_like(hn_ref[0]))
    xa_ref[...] = _dot(hext_ref[...], win_ref[:, 0:POOL_WIDTH])

    qh = [_rot(q[:, hd * DK:(hd + 1) * DK], c, s) for hd in range(HEADS)]
    sc = [_dot_nt(qh[hd].astype(BF16), k_ref[0, :, hd * DK:(hd + 1) * DK]) for hd in range(HEADS)]

    xg = _dot(h, win_ref[:, OFF_G:OFF_GATE])

    ys = []
    for hd in range(HEADS):
        vb = v_ref[0, :, hd * DV:(hd + 1) * DV]
        S = S_ref[hd]
        qdec = (jnp.concatenate([qh[hd], qh[hd]], axis=-1) * qd_ref[hd]).astype(BF16)
        states = jnp.concatenate([S.astype(BF16), sb_ref[0, 0, hd]], axis=0)
        ys.append(_dot((sc[hd] * mask_ref[hd]).astype(BF16), vb) + _dot(qdec, states))
        S_ref[hd] = S * cdf_ref[hd][0:1, :] + _dot_tn(kf_ref[0, :, hd * DK:(hd + 1) * DK], vb)

    gz = _dot(h, win_ref[:, OFF_GATE:])

    tglob = i * T + lax.broadcasted_iota(jnp.int32, (T, POOL_GROUP_DIM), 0)
    pooled = []
    for gi, w in enumerate(POOL_WINDOWS):
        cols = slice(gi * POOL_GROUP_DIM, (gi + 1) * POOL_GROUP_DIM)
        acc = xa_ref[HALO - w // 2:HALO - w // 2 + T, cols]
        for kk in range(-w // 2 + 1, w // 2):
            acc = acc + xa_ref[HALO + kk:HALO + kk + T, cols]
        lo = jnp.maximum(tglob - w // 2, 0)
        hi = jnp.minimum(tglob + w // 2, seq_len)
        mean = acc / (hi - lo).astype(F32)
        d = mean - xa_ref[HALO:HALO + T, cols]
        pooled.append(_dot(d.astype(BF16), pw_ref[gi]))
    pooled = jnp.concatenate(pooled, axis=-1) * ps_ref[...]
    a = _dot(pooled.astype(BF16), wpo_ref[...])

    rin = []
    for hd in range(HEADS):
        y = ys[hd]
        yc = y - jnp.mean(y, axis=-1, keepdims=True)
        yn = yc * lax.rsqrt(jnp.mean(yc * yc, axis=-1, keepdims=True) + GN_EPS)
        gh = xg[:, hd * DV:(hd + 1) * DV]
        rin.append((yn * (gh * jax.nn.sigmoid(gh))).astype(BF16))
    r = _dot(jnp.concatenate(rin, axis=-1), wro_ref[...])

    gates = jax.nn.sigmoid(gz)
    merged = gates[:, :D_MODEL] * a + gates[:, D_MODEL:] * r
    o_ref[0] = x_ref[0] + _dot(merged.astype(BF16), wo_ref[...])


def _ffn_kernel(x_ref, p_ref, gf_ref, wfi_ref, wfo_ref, gp_ref, wpg_ref, wpp_ref, gfin_ref, o_ref,
                *, final_norm):
    rpg = x_ref.shape[1] // FFN_GROUPS
    groups = [slice(gi * rpg, (gi + 1) * rpg) for gi in range(FFN_GROUPS)]
    xs = [x_ref[0, rows, :] for rows in groups]
    gt, up, pp = [], [], []
    for x, rows in zip(xs, groups):
        h2 = _rms(x, gf_ref[...]).astype(BF16)
        gt.append(_dot(h2, wfi_ref[:, :D_FF]))
        up.append(_dot(h2, wfi_ref[:, D_FF:]))
        pp.append(_dot(p_ref[0, rows, :].astype(BF16), wpp_ref[...]))
    xs = [x + _dot((g * jax.nn.sigmoid(g) * u).astype(BF16), wfo_ref[...])
          for x, g, u in zip(xs, gt, up)]
    pg = [jax.nn.sigmoid(_dot(_rms(x, gp_ref[...]).astype(BF16), wpg_ref[...])) for x in xs]
    for x, rows, proj, gate in zip(xs, groups, pp, pg):
        x = x + proj * gate
        if final_norm:
            x = _rms(x, gfin_ref[...])
        o_ref[0, rows, :] = x


def _layer_spec(arr, layer):
    nd = arr.ndim - 1
    return pl.BlockSpec((None,) + arr.shape[1:], lambda b, i: (layer,) + (0,) * nd,
                        pipeline_mode=pl.Buffered(1))


def _const_spec(arr):
    nd = arr.ndim
    return pl.BlockSpec(arr.shape, lambda b, i: (0,) * nd, pipeline_mode=pl.Buffered(1))


def _params(semantics):
    return pltpu.CompilerParams(dimension_semantics=semantics, vmem_limit_bytes=VMEM_LIMIT)


def _kv(x, layer, g_mix, w_in, cos_t, sin_t, dl):
    B, L, D = x.shape
    T = T_KV
    nt = L // T
    cpt = T // CHUNK
    rev = lambda b, i: (b, nt - 1 - i, 0)
    return pl.pallas_call(
        _kv_kernel,
        grid=(B, nt),
        in_specs=[
            pl.BlockSpec((1, T, D), rev),
            _layer_spec(g_mix, layer),
            _layer_spec(w_in, layer),
            pl.BlockSpec((T, DK), lambda b, i: (nt - 1 - i, 0)),
            pl.BlockSpec((T, DK), lambda b, i: (nt - 1 - i, 0)),
            _layer_spec(dl, layer),
        ],
        out_specs=[
            pl.BlockSpec((1, T, D), rev),
            pl.BlockSpec((1, T, QK_WIDTH), rev),
            pl.BlockSpec((1, T, QK_WIDTH), rev),
            pl.BlockSpec((1, T, V_WIDTH), rev),
            pl.BlockSpec((1, cpt, HEADS, DK, DV), lambda b, i: (b, nt - 1 - i, 0, 0, 0)),
        ],
        out_shape=[
            jax.ShapeDtypeStruct((B, L, D), BF16),
            jax.ShapeDtypeStruct((B, L, QK_WIDTH), BF16),
            jax.ShapeDtypeStruct((B, L, QK_WIDTH), BF16),
            jax.ShapeDtypeStruct((B, L, V_WIDTH), BF16),
            jax.ShapeDtypeStruct((B, L // CHUNK, HEADS, DK, DV), BF16),
        ],
        scratch_shapes=[
            pltpu.VMEM((HEADS, DK, DV), F32),
            pltpu.VMEM((HEADS, CHUNK, DK), F32),
            pltpu.VMEM((HEADS, CHUNK, DK), F32),
            pltpu.VMEM((HEADS, 8, DV), F32),
        ],
        compiler_params=_params(("arbitrary", "arbitrary")),
        name="kv",
    )(x, g_mix, w_in, cos_t, sin_t, dl)


def _mixer(x, h, k, kf, v, sb, layer, w_in, pool_w, pool_scale, w_pool_out, w_ret_out, w_o,
           cos_t, sin_t, dl):
    B, L, D = x.shape
    T = CHUNK
    nt = L // T
    hb = T // HALO
    nhb = L // HALO
    tile = lambda b, i: (b, i, 0)
    return pl.pallas_call(
        functools.partial(_mixer_kernel, seq_len=L),
        grid=(B, nt),
        in_specs=[
            pl.BlockSpec((1, T, D), tile),
            pl.BlockSpec((1, T, D), tile),
            pl.BlockSpec((1, HALO, D), lambda b, i: (b, jnp.maximum(i * hb - 1, 0), 0)),
            pl.BlockSpec((1, HALO, D), lambda b, i: (b, jnp.minimum((i + 1) * hb, nhb - 1), 0)),
            pl.BlockSpec((1, T, QK_WIDTH), tile),
            pl.BlockSpec((1, T, QK_WIDTH), tile),
            pl.BlockSpec((1, T, V_WIDTH), tile),
            pl.BlockSpec((1, 1, HEADS, DK, DV), lambda b, i: (b, i, 0, 0, 0)),
            pl.BlockSpec((T, DK), lambda b, i: (i, 0)),
            pl.BlockSpec((T, DK), lambda b, i: (i, 0)),
            _layer_spec(dl, layer),
            _layer_spec(w_in, layer),
            _layer_spec(pool_w, layer),
            _layer_spec(pool_scale, layer),
            _layer_spec(w_pool_out, layer),
            _layer_spec(w_ret_out, layer),
            _layer_spec(w_o, layer),
        ],
        out_specs=pl.BlockSpec((1, T, D), tile),
        out_shape=jax.ShapeDtypeStruct((B, L, D), F32),
        scratch_shapes=[
            pltpu.VMEM((HEADS, DK, DV), F32),
            pltpu.VMEM((HEADS, T, T), F32),
            pltpu.VMEM((HEADS, T, 2 * DK), F32),
            pltpu.VMEM((HEADS, 8, DV), F32),
            pltpu.VMEM((T + 2 * HALO, D), BF16),
            pltpu.VMEM((T + 2 * HALO, POOL_WIDTH), F32),
        ],
        compiler_params=_params(("arbitrary", "arbitrary")),
        name="mixer",
    )(x, h, h, h, k, kf, v, sb, cos_t, sin_t, dl, w_in, pool_w, pool_scale, w_pool_out, w_ret_out, w_o)


def _ffn(x, p, layer, g_ffn, w_ffn_in, w_ffn_out, g_ple, w_ple_gate, w_ple_proj, g_final, final_norm):
    B, L, D = x.shape
    T = T_FFN
    tile = lambda b, i: (b, i, 0)
    return pl.pallas_call(
        functools.partial(_ffn_kernel, final_norm=final_norm),
        grid=(B, L // T),
        in_specs=[
            pl.BlockSpec((1, T, D), tile),
            pl.BlockSpec((None, 1, T, PLE_DIM), lambda b, i: (layer, b, i, 0)),
            _layer_spec(g_ffn, layer),
            _layer_spec(w_ffn_in, layer),
            _layer_spec(w_ffn_out, layer),
            _layer_spec(g_ple, layer),
            _layer_spec(w_ple_gate, layer),
            _layer_spec(w_ple_proj, layer),
            _const_spec(g_final),
        ],
        out_specs=pl.BlockSpec((1, T, D), tile),
        out_shape=jax.ShapeDtypeStruct((B, L, D), F32),
        compiler_params=_params(("parallel", "parallel")),
        name="ffn",
    )(x, p, g_ffn, w_ffn_in, w_ffn_out, g_ple, w_ple_gate, w_ple_proj, g_final)


def _rotary_tables(L):
    half = DK // 2
    inv = 1.0 / (ROPE_BASE ** (jnp.arange(half, dtype=F32) / half))
    ang = jnp.arange(L).astype(F32)[:, None] * inv[None, :]
    cos = jnp.cos(ang)
    sin = jnp.sin(ang)
    return jnp.concatenate([cos, cos], axis=-1), jnp.concatenate([-sin, sin], axis=-1)


def kernel(x_prompt, x_sample, p_prompt, p_sample, g_mix, w_in, pool_w, pool_scale, decay_logit,
           w_pool_out, w_ret_out, w_o, g_ffn, w_ffn_in, w_ffn_out, g_ple, w_ple_gate, w_ple_proj,
           g_final):
    depth = w_in.shape[0]
    L = x_prompt.shape[1]
    cos_t, sin_t = _rotary_tables(L)
    w_in_b = w_in.astype(BF16)
    pool_w_b = pool_w.astype(BF16)
    w_pool_out_b = w_pool_out.astype(BF16)
    w_ret_out_b = w_ret_out.astype(BF16)
    w_o_b = w_o.astype(BF16)
    w_ffn_in_b = w_ffn_in.astype(BF16)
    w_ffn_out_b = w_ffn_out.astype(BF16)
    w_ple_gate_b = w_ple_gate.astype(BF16)
    w_ple_proj_b = w_ple_proj.astype(BF16)
    dl = jnp.broadcast_to(decay_logit.astype(F32)[..., None, None], decay_logit.shape + (1, DV))
    rows = lambda v: v.reshape(v.shape[0], 1, v.shape[1])
    g_mix_r, g_ffn_r, g_ple_r, pool_scale_r = rows(g_mix), rows(g_ffn), rows(g_ple), rows(pool_scale)
    g_final_r = g_final.reshape(1, -1)

    def trunk(x, p):
        for l in range(depth):
            h, k, kf, v, sb = _kv(x, l, g_mix_r, w_in_b, cos_t, sin_t, dl)
            x = _mixer(x, h, k, kf, v, sb, l, w_in_b, pool_w_b, pool_scale_r, w_pool_out_b,
                       w_ret_out_b, w_o_b, cos_t, sin_t, dl)
            x = _ffn(x, p, l, g_ffn_r, w_ffn_in_b, w_ffn_out_b, g_ple_r, w_ple_gate_b,
                     w_ple_proj_b, g_final_r, l == depth - 1)
        return x

    return trunk(x_prompt, p_prompt), trunk(x_sample, p_sample)
```

```python
import functools

import jax
import jax.numpy as jnp
from jax import lax
from jax.experimental import pallas as pl
from jax.experimental.pallas import tpu as pltpu

F32 = jnp.float32
BF16 = jnp.bfloat16

D_MODEL = 1024
PLE_DIM = 256
POOL_GROUP_DIM = 128
POOL_WIDTH = 512
POOL_WINDOWS = (2, 4, 8, 16)
HEADS = 4
DK = 128
DV = 256
QK_WIDTH = HEADS * DK
V_WIDTH = HEADS * DV
D_FF = 2816
ROPE_BASE = 10000.0
RMS_EPS = 1e-6
GN_EPS = 1e-5

OFF_Q = POOL_WIDTH
OFF_K = OFF_Q + QK_WIDTH
OFF_V = OFF_K + QK_WIDTH
OFF_G = OFF_V + V_WIDTH
OFF_GATE = OFF_G + V_WIDTH

HALO = 16
CHUNK = 256
T_KV = 512
T_MIX = 512
T_FFN = 1024
FFN_GROUPS = 4
VMEM_LIMIT = 56 * 1024 * 1024
FWD, BWD = 0, 1


def _rms(x, g):
    return x * lax.rsqrt(jnp.mean(x * x, axis=-1, keepdims=True) + RMS_EPS) * g


def _rot(x, c, s):
    return x * c + pltpu.roll(x, DK // 2, 1) * s


def _dot(a, b):
    return jnp.dot(a, b, preferred_element_type=F32)


def _dot_tn(a, b):
    return lax.dot_general(a, b, (((0,), (0,)), ((), ())), preferred_element_type=F32)


def _dot_nt(a, b):
    return lax.dot_general(a, b, (((1,), (1,)), ((), ())), preferred_element_type=F32)


def _log_decay(dl_ref, direction, h):
    x = dl_ref[direction, h]
    return jnp.minimum(x, 0.0) - jnp.log1p(jnp.exp(-jnp.abs(x)))


def _kv_kernel(x_ref, g_ref, w_ref, cos_ref, sin_ref, dl_ref,
               h_ref, k_ref, kf_ref, v_ref, sb_ref,
               S_ref, kdf_ref, kdb_ref, cdb_ref):
    b = pl.program_id(0)
    i = pl.program_id(1)

    @pl.when((b == 0) & (i == 0))
    def _():
        r = lax.broadcasted_iota(jnp.int32, (CHUNK, DK), 0).astype(F32)
        for hd in range(HEADS):
            lgf = _log_decay(dl_ref, FWD, hd)
            lgb = _log_decay(dl_ref, BWD, hd)
            kdf_ref[hd] = jnp.exp((CHUNK - 1.0 - r) * lgf[:, :DK]) * DK ** -0.5
            kdb_ref[hd] = jnp.exp(r * lgb[:, :DK]) * DK ** -0.5
            cdb_ref[hd] = jnp.broadcast_to(jnp.exp(float(CHUNK) * lgb), (8, DV))

    @pl.when(i == 0)
    def _():
        S_ref[...] = jnp.zeros_like(S_ref)

    chunks = [slice(ci * CHUNK, (ci + 1) * CHUNK) for ci in reversed(range(x_ref.shape[1] // CHUNK))]
    ks, vs = [], []
    for rows in chunks:
        h = _rms(x_ref[0, rows, :], g_ref[...]).astype(BF16)
        h_ref[0, rows, :] = h
        ks.append(_dot(h, w_ref[:, OFF_K:OFF_V]))
        v = _dot(h, w_ref[:, OFF_V:OFF_G]).astype(BF16)
        v_ref[0, rows, :] = v
        vs.append(v)
    for rows, k, v in zip(chunks, ks, vs):
        c = cos_ref[rows, :]
        s = sin_ref[rows, :]
        ci = rows.start // CHUNK
        for hd in range(HEADS):
            kh = _rot(k[:, hd * DK:(hd + 1) * DK], c, s)
            k_ref[0, rows, hd * DK:(hd + 1) * DK] = kh.astype(BF16)
            kf_ref[0, rows, hd * DK:(hd + 1) * DK] = (kh * kdf_ref[hd]).astype(BF16)
            S = S_ref[hd]
            sb_ref[0, ci, hd] = S.astype(BF16)
            upd = _dot_tn((kh * kdb_ref[hd]).astype(BF16), v[:, hd * DV:(hd + 1) * DV])
            S_ref[hd] = S * cdb_ref[hd][0:1, :] + upd


def _mixer_kernel(x_ref, h_ref, hp_ref, hn_ref, k_ref, kf_ref, v_ref, sb_ref, cos_ref, sin_ref, dl_ref,
                  win_ref, pw_ref, ps_ref, wpo_ref, wro_ref, wo_ref, o_ref,
                  S_ref, mask_ref, qd_ref, cdf_ref, hext_ref, xa_ref, *, seq_len):
    b = pl.program_id(0)
    i = pl.program_id(1)
    nt = pl.num_programs(1)
    T = x_ref.shape[1]
    C = CHUNK
    chunks = [slice(ci * C, (ci + 1) * C) for ci in range(T // C)]
    heads = [slice(hd * DK, (hd + 1) * DK) for hd in range(HEADS)]
    vheads = [slice(hd * DV, (hd + 1) * DV) for hd in range(HEADS)]

    @pl.when((b == 0) & (i == 0))
    def _():
        ri = lax.broadcasted_iota(jnp.int32, (C, C), 0).astype(F32)
        ci = lax.broadcasted_iota(jnp.int32, (C, C), 1).astype(F32)
        r = lax.broadcasted_iota(jnp.int32, (C, DK), 0).astype(F32)
        for hd in range(HEADS):
            lgf = _log_decay(dl_ref, FWD, hd)
            lgb = _log_decay(dl_ref, BWD, hd)
            dfw = ri - ci
            m = jnp.where(dfw >= 0.0,
                          jnp.exp(lgf[:, :C] * jnp.maximum(dfw, 0.0)),
                          jnp.exp(lgb[:, :C] * jnp.maximum(-dfw, 0.0)))
            mask_ref[hd] = m * DK ** -0.5
            qd_ref[hd, :, 0:DK] = jnp.exp((r + 1.0) * lgf[:, :DK])
            qd_ref[hd, :, DK:2 * DK] = jnp.exp((C - r) * lgb[:, :DK])
            cdf_ref[hd] = jnp.broadcast_to(jnp.exp(float(C) * lgf), (8, DV))

    @pl.when(i == 0)
    def _():
        S_ref[...] = jnp.zeros_like(S_ref)

    h = h_ref[0]
    c = cos_ref[...]
    s = sin_ref[...]
    q = _dot(h, win_ref[:, OFF_Q:OFF_K])

    hext_ref[0:HALO, :] = jnp.where(i > 0, hp_ref[0], jnp.zeros_like(hp_ref[0]))
    hext_ref[HALO:HALO + T, :] = h
    hext_ref[HALO + T:, :] = jnp.where(i < nt - 1, hn_ref[0], jnp.zeros_like(hn_ref[0]))
    xa_ref[...] = _dot(hext_ref[...], win_ref[:, 0:POOL_WIDTH])

    qh = [[_rot(q[rows, hs], c[rows], s[rows]) for hs in heads] for rows in chunks]
    sc = [[_dot_nt(qh[ci][hd].astype(BF16), k_ref[0, rows, heads[hd]]) for hd in range(HEADS)]
          for ci, rows in enumerate(chunks)]

    xg = _dot(h, win_ref[:, OFF_G:OFF_GATE])

    ys = []
    for ci, rows in enumerate(chunks):
        ys.append([])
        for hd in range(HEADS):
            vb = v_ref[0, rows, vheads[hd]]
            S = S_ref[hd]
            qdec = (jnp.concatenate([qh[ci][hd], qh[ci][hd]], axis=-1) * qd_ref[hd]).astype(BF16)
            states = jnp.concatenate([S.astype(BF16), sb_ref[0, ci, hd]], axis=0)
            ys[ci].append(_dot((sc[ci][hd] * mask_ref[hd]).astype(BF16), vb) + _dot(qdec, states))
            S_ref[hd] = S * cdf_ref[hd][0:1, :] + _dot_tn(kf_ref[0, rows, heads[hd]], vb)

    gz = _dot(h, win_ref[:, OFF_GATE:])

    tglob = i * T + lax.broadcasted_iota(jnp.int32, (T, POOL_GROUP_DIM), 0)
    pooled = []
    for gi, w in enumerate(POOL_WINDOWS):
        cols = slice(gi * POOL_GROUP_DIM, (gi + 1) * POOL_GROUP_DIM)
        acc = xa_ref[HALO - w // 2:HALO - w // 2 + T, cols]
        for kk in range(-w // 2 + 1, w // 2):
            acc = acc + xa_ref[HALO + kk:HALO + kk + T, cols]
        lo = jnp.maximum(tglob - w // 2, 0)
        hi = jnp.minimum(tglob + w // 2, seq_len)
        mean = acc / (hi - lo).astype(F32)
        d = mean - xa_ref[HALO:HALO + T, cols]
        pooled.append(_dot(d.astype(BF16), pw_ref[gi]))
    pooled = jnp.concatenate(pooled, axis=-1) * ps_ref[...]
    a = _dot(pooled.astype(BF16), wpo_ref[...])

    rs = []
    for ci, rows in enumerate(chunks):
        rin = []
        for hd in range(HEADS):
            y = ys[ci][hd]
            yc = y - jnp.mean(y, axis=-1, keepdims=True)
            yn = yc * lax.rsqrt(jnp.mean(yc * yc, axis=-1, keepdims=True) + GN_EPS)
            gh = xg[rows, vheads[hd]]
            rin.append((yn * (gh * jax.nn.sigmoid(gh))).astype(BF16))
        rs.append(_dot(jnp.concatenate(rin, axis=-1), wro_ref[...]))

    for rows, r in zip(chunks, rs):
        gates = jax.nn.sigmoid(gz[rows])
        merged = gates[:, :D_MODEL] * a[rows] + gates[:, D_MODEL:] * r
        o_ref[0, rows, :] = x_ref[0, rows, :] + _dot(merged.astype(BF16), wo_ref[...])


def _ffn_kernel(x_ref, p_ref, gf_ref, wfi_ref, wfo_ref, gp_ref, wpg_ref, wpp_ref, gfin_ref, o_ref,
                *, final_norm):
    rpg = x_ref.shape[1] // FFN_GROUPS
    groups = [slice(gi * rpg, (gi + 1) * rpg) for gi in range(FFN_GROUPS)]
    xs = [x_ref[0, rows, :] for rows in groups]
    gt, up, pp = [], [], []
    for x, rows in zip(xs, groups):
        h2 = _rms(x, gf_ref[...]).astype(BF16)
        gt.append(_dot(h2, wfi_ref[:, :D_FF]))
        up.append(_dot(h2, wfi_ref[:, D_FF:]))
        pp.append(_dot(p_ref[0, rows, :].astype(BF16), wpp_ref[...]))
    xs = [x + _dot((g * jax.nn.sigmoid(g) * u).astype(BF16), wfo_ref[...])
          for x, g, u in zip(xs, gt, up)]
    pg = [jax.nn.sigmoid(_dot(_rms(x, gp_ref[...]).astype(BF16), wpg_ref[...])) for x in xs]
    for x, rows, proj, gate in zip(xs, groups, pp, pg):
        x = x + proj * gate
        if final_norm:
            x = _rms(x, gfin_ref[...])
        o_ref[0, rows, :] = x


def _layer_spec(arr, layer):
    nd = arr.ndim - 1
    return pl.BlockSpec((None,) + arr.shape[1:], lambda b, i: (layer,) + (0,) * nd,
                        pipeline_mode=pl.Buffered(1))


def _const_spec(arr):
    nd = arr.ndim
    return pl.BlockSpec(arr.shape, lambda b, i: (0,) * nd, pipeline_mode=pl.Buffered(1))


def _params(semantics):
    return pltpu.CompilerParams(dimension_semantics=semantics, vmem_limit_bytes=VMEM_LIMIT)


def _kv(x, layer, g_mix, w_in, cos_t, sin_t, dl):
    B, L, D = x.shape
    T = T_KV
    nt = L // T
    cpt = T // CHUNK
    rev = lambda b, i: (b, nt - 1 - i, 0)
    return pl.pallas_call(
        _kv_kernel,
        grid=(B, nt),
        in_specs=[
            pl.BlockSpec((1, T, D), rev),
            _layer_spec(g_mix, layer),
            _layer_spec(w_in, layer),
            pl.BlockSpec((T, DK), lambda b, i: (nt - 1 - i, 0)),
            pl.BlockSpec((T, DK), lambda b, i: (nt - 1 - i, 0)),
            _layer_spec(dl, layer),
        ],
        out_specs=[
            pl.BlockSpec((1, T, D), rev),
            pl.BlockSpec((1, T, QK_WIDTH), rev),
            pl.BlockSpec((1, T, QK_WIDTH), rev),
            pl.BlockSpec((1, T, V_WIDTH), rev),
            pl.BlockSpec((1, cpt, HEADS, DK, DV), lambda b, i: (b, nt - 1 - i, 0, 0, 0)),
        ],
        out_shape=[
            jax.ShapeDtypeStruct((B, L, D), BF16),
            jax.ShapeDtypeStruct((B, L, QK_WIDTH), BF16),
            jax.ShapeDtypeStruct((B, L, QK_WIDTH), BF16),
            jax.ShapeDtypeStruct((B, L, V_WIDTH), BF16),
            jax.ShapeDtypeStruct((B, L // CHUNK, HEADS, DK, DV), BF16),
        ],
        scratch_shapes=[
            pltpu.VMEM((HEADS, DK, DV), F32),
            pltpu.VMEM((HEADS, CHUNK, DK), F32),
            pltpu.VMEM((HEADS, CHUNK, DK), F32),
            pltpu.VMEM((HEADS, 8, DV), F32),
        ],
        compiler_params=_params(("arbitrary", "arbitrary")),
        name="kv",
    )(x, g_mix, w_in, cos_t, sin_t, dl)


def _mixer(x, h, k, kf, v, sb, layer, w_in, pool_w, pool_scale, w_pool_out, w_ret_out, w_o,
           cos_t, sin_t, dl):
    B, L, D = x.shape
    T = T_MIX
    nt = L // T
    cpt = T // CHUNK
    hb = T // HALO
    nhb = L // HALO
    tile = lambda b, i: (b, i, 0)
    return pl.pallas_call(
        functools.partial(_mixer_kernel, seq_len=L),
        grid=(B, nt),
        in_specs=[
            pl.BlockSpec((1, T, D), tile),
            pl.BlockSpec((1, T, D), tile),
            pl.BlockSpec((1, HALO, D), lambda b, i: (b, jnp.maximum(i * hb - 1, 0), 0)),
            pl.BlockSpec((1, HALO, D), lambda b, i: (b, jnp.minimum((i + 1) * hb, nhb - 1), 0)),
            pl.BlockSpec((1, T, QK_WIDTH), tile),
            pl.BlockSpec((1, T, QK_WIDTH), tile),
            pl.BlockSpec((1, T, V_WIDTH), tile),
            pl.BlockSpec((1, cpt, HEADS, DK, DV), lambda b, i: (b, i, 0, 0, 0)),
            pl.BlockSpec((T, DK), lambda b, i: (i, 0)),
            pl.BlockSpec((T, DK), lambda b, i: (i, 0)),
            _layer_spec(dl, layer),
            _layer_spec(w_in, layer),
            _layer_spec(pool_w, layer),
            _layer_spec(pool_scale, layer),
            _layer_spec(w_pool_out, layer),
            _layer_spec(w_ret_out, layer),
            _layer_spec(w_o, layer),
        ],
        out_specs=pl.BlockSpec((1, T, D), tile),
        out_shape=jax.ShapeDtypeStruct((B, L, D), F32),
        scratch_shapes=[
            pltpu.VMEM((HEADS, DK, DV), F32),
            pltpu.VMEM((HEADS, CHUNK, CHUNK), F32),
            pltpu.VMEM((HEADS, CHUNK, 2 * DK), F32),
            pltpu.VMEM((HEADS, 8, DV), F32),
            pltpu.VMEM((T + 2 * HALO, D), BF16),
            pltpu.VMEM((T + 2 * HALO, POOL_WIDTH), F32),
        ],
        compiler_params=_params(("arbitrary", "arbitrary")),
        name="mixer",
    )(x, h, h, h, k, kf, v, sb, cos_t, sin_t, dl, w_in, pool_w, pool_scale, w_pool_out, w_ret_out, w_o)


def _ffn(x, p, layer, g_ffn, w_ffn_in, w_ffn_out, g_ple, w_ple_gate, w_ple_proj, g_final, final_norm):
    B, L, D = x.shape
    T = T_FFN
    tile = lambda b, i: (b, i, 0)
    return pl.pallas_call(
        functools.partial(_ffn_kernel, final_norm=final_norm),
        grid=(B, L // T),
        in_specs=[
            pl.BlockSpec((1, T, D), tile),
            pl.BlockSpec((None, 1, T, PLE_DIM), lambda b, i: (layer, b, i, 0)),
            _layer_spec(g_ffn, layer),
            _layer_spec(w_ffn_in, layer),
            _layer_spec(w_ffn_out, layer),
            _layer_spec(g_ple, layer),
            _layer_spec(w_ple_gate, layer),
            _layer_spec(w_ple_proj, layer),
            _const_spec(g_final),
        ],
        out_specs=pl.BlockSpec((1, T, D), tile),
        out_shape=jax.ShapeDtypeStruct((B, L, D), F32),
        compiler_params=_params(("parallel", "parallel")),
        name="ffn",
    )(x, p, g_ffn, w_ffn_in, w_ffn_out, g_ple, w_ple_gate, w_ple_proj, g_final)


def _rotary_tables(L):
    half = DK // 2
    inv = 1.0 / (ROPE_BASE ** (jnp.arange(half, dtype=F32) / half))
    base = (jnp.arange(L // CHUNK) * CHUNK).astype(F32)[:, None, None] * inv
    off = jnp.arange(CHUNK).astype(F32)[None, :, None] * inv
    cb, sb, co, so = jnp.cos(base), jnp.sin(base), jnp.cos(off), jnp.sin(off)
    cos = (cb * co - sb * so).reshape(L, half)
    sin = (sb * co + cb * so).reshape(L, half)
    return jnp.concatenate([cos, cos], axis=-1), jnp.concatenate([-sin, sin], axis=-1)


def kernel(x_prompt, x_sample, p_prompt, p_sample, g_mix, w_in, pool_w, pool_scale, decay_logit,
           w_pool_out, w_ret_out, w_o, g_ffn, w_ffn_in, w_ffn_out, g_ple, w_ple_gate, w_ple_proj,
           g_final):
    depth = w_in.shape[0]
    L = x_prompt.shape[1]
    cos_t, sin_t = _rotary_tables(L)
    w_in_b = w_in.astype(BF16)
    pool_w_b = pool_w.astype(BF16)
    w_pool_out_b = w_pool_out.astype(BF16)
    w_ret_out_b = w_ret_out.astype(BF16)
    w_o_b = w_o.astype(BF16)
    w_ffn_in_b = w_ffn_in.astype(BF16)
    w_ffn_out_b = w_ffn_out.astype(BF16)
    w_ple_gate_b = w_ple_gate.astype(BF16)
    w_ple_proj_b = w_ple_proj.astype(BF16)
    dl = jnp.broadcast_to(decay_logit.astype(F32)[..., None, None], decay_logit.shape + (1, DV))
    rows = lambda v: v.reshape(v.shape[0], 1, v.shape[1])
    g_mix_r, g_ffn_r, g_ple_r, pool_scale_r = rows(g_mix), rows(g_ffn), rows(g_ple), rows(pool_scale)
    g_final_r = g_final.reshape(1, -1)

    def trunk(x, p):
        for l in range(depth):
            h, k, kf, v, sb = _kv(x, l, g_mix_r, w_in_b, cos_t, sin_t, dl)
            x = _mixer(x, h, k, kf, v, sb, l, w_in_b, pool_w_b, pool_scale_r, w_pool_out_b,
                       w_ret_out_b, w_o_b, cos_t, sin_t, dl)
            x = _ffn(x, p, l, g_ffn_r, w_ffn_in_b, w_ffn_out_b, g_ple_r, w_ple_gate_b,
                     w_ple_proj_b, g_final_r, l == depth - 1)
        return x

    return trunk(x_prompt, p_prompt), trunk(x_sample, p_sample)
```

```python
import functools

import jax
import jax.numpy as jnp
from jax import lax
from jax.experimental import pallas as pl
from jax.experimental.pallas import tpu as pltpu

F32 = jnp.float32
BF16 = jnp.bfloat16

D_MODEL = 1024
PLE_DIM = 256
POOL_GROUP_DIM = 128
POOL_WIDTH = 512
POOL_WINDOWS = (2, 4, 8, 16)
HEADS = 4
DK = 128
DV = 256
QK_WIDTH = HEADS * DK
V_WIDTH = HEADS * DV
D_FF = 2816
ROPE_BASE = 10000.0
RMS_EPS = 1e-6
GN_EPS = 1e-5

OFF_Q = POOL_WIDTH
OFF_K = OFF_Q + QK_WIDTH
OFF_V = OFF_K + QK_WIDTH
OFF_G = OFF_V + V_WIDTH
OFF_GATE = OFF_G + V_WIDTH

HALO = 16
CHUNK = 256
T_KV = 1024
T_MIX = 512
T_FFN = 1024
FFN_GROUPS = 4
VMEM_LIMIT = 56 * 1024 * 1024
FWD, BWD = 0, 1


def _rms(x, g):
    return x * lax.rsqrt(jnp.mean(x * x, axis=-1, keepdims=True) + RMS_EPS) * g


def _rot(x, c, s):
    return x * c + pltpu.roll(x, DK // 2, 1) * s


def _rotary_chunk(bt_ref, ot_ref, ci):
    cb, sb, sbs = bt_ref[ci, 0:1, :], bt_ref[ci, 1:2, :], bt_ref[ci, 2:3, :]
    co, so, sos = ot_ref[0], ot_ref[1], ot_ref[2]
    return cb * co - sb * so, sbs * co + cb * sos


def _dot(a, b):
    return jnp.dot(a, b, preferred_element_type=F32)


def _dot_tn(a, b):
    return lax.dot_general(a, b, (((0,), (0,)), ((), ())), preferred_element_type=F32)


def _dot_nt(a, b):
    return lax.dot_general(a, b, (((1,), (1,)), ((), ())), preferred_element_type=F32)


def _log_decay(dl_ref, direction, h):
    x = dl_ref[direction, h]
    return jnp.minimum(x, 0.0) - jnp.log1p(jnp.exp(-jnp.abs(x)))


def _kv_kernel(x_ref, g_ref, w_ref, bt_ref, ot_ref, dl_ref,
               h_ref, k_ref, v_ref, sb_ref,
               S_ref, kdb_ref, cdb_ref):
    b = pl.program_id(0)
    i = pl.program_id(1)

    @pl.when((b == 0) & (i == 0))
    def _():
        r = lax.broadcasted_iota(jnp.int32, (CHUNK, DK), 0).astype(F32)
        for hd in range(HEADS):
            lgb = _log_decay(dl_ref, BWD, hd)
            kdb_ref[hd] = jnp.exp(r * lgb[:, :DK]) * DK ** -0.5
            cdb_ref[hd] = jnp.broadcast_to(jnp.exp(float(CHUNK) * lgb), (8, DV))

    @pl.when(i == 0)
    def _():
        S_ref[...] = jnp.zeros_like(S_ref)

    chunks = [slice(ci * CHUNK, (ci + 1) * CHUNK) for ci in reversed(range(x_ref.shape[1] // CHUNK))]
    ks, vs = [], []
    for rows in chunks:
        h = _rms(x_ref[0, rows, :], g_ref[...]).astype(BF16)
        h_ref[0, rows, :] = h
        ks.append(_dot(h, w_ref[:, OFF_K:OFF_V]))
        v = _dot(h, w_ref[:, OFF_V:OFF_G]).astype(BF16)
        v_ref[0, rows, :] = v
        vs.append(v)
    for rows, k, v in zip(chunks, ks, vs):
        ci = rows.start // CHUNK
        c, s = _rotary_chunk(bt_ref, ot_ref, ci)
        for hd in range(HEADS):
            kh = _rot(k[:, hd * DK:(hd + 1) * DK], c, s)
            k_ref[0, rows, hd * DK:(hd + 1) * DK] = kh.astype(BF16)
            S = S_ref[hd]
            sb_ref[0, ci, hd] = S.astype(BF16)
            upd = _dot_tn((kh * kdb_ref[hd]).astype(BF16), v[:, hd * DV:(hd + 1) * DV])
            S_ref[hd] = S * cdb_ref[hd][0:1, :] + upd


def _mixer_kernel(x_ref, h_ref, hp_ref, hn_ref, k_ref, v_ref, sb_ref, bt_ref, ot_ref, dl_ref,
                  win_ref, pw_ref, ps_ref, wpo_ref, wro_ref, wo_ref, o_ref,
                  S_ref, mask_ref, qd_ref, kdf_ref, cdf_ref, hext_ref, xa_ref, *, seq_len):
    b = pl.program_id(0)
    i = pl.program_id(1)
    nt = pl.num_programs(1)
    T = x_ref.shape[1]
    C = CHUNK
    chunks = [slice(ci * C, (ci + 1) * C) for ci in range(T // C)]
    heads = [slice(hd * DK, (hd + 1) * DK) for hd in range(HEADS)]
    vheads = [slice(hd * DV, (hd + 1) * DV) for hd in range(HEADS)]

    @pl.when((b == 0) & (i == 0))
    def _():
        ri = lax.broadcasted_iota(jnp.int32, (C, C), 0).astype(F32)
        ci = lax.broadcasted_iota(jnp.int32, (C, C), 1).astype(F32)
        r = lax.broadcasted_iota(jnp.int32, (C, DK), 0).astype(F32)
        for hd in range(HEADS):
            lgf = _log_decay(dl_ref, FWD, hd)
            lgb = _log_decay(dl_ref, BWD, hd)
            dfw = ri - ci
            m = jnp.where(dfw >= 0.0,
                          jnp.exp(lgf[:, :C] * jnp.maximum(dfw, 0.0)),
                          jnp.exp(lgb[:, :C] * jnp.maximum(-dfw, 0.0)))
            mask_ref[hd] = m * DK ** -0.5
            qd_ref[hd, :, 0:DK] = jnp.exp((r + 1.0) * lgf[:, :DK])
            qd_ref[hd, :, DK:2 * DK] = jnp.exp((C - r) * lgb[:, :DK])
            kdf_ref[hd] = jnp.exp((C - 1.0 - r) * lgf[:, :DK]) * DK ** -0.5
            cdf_ref[hd] = jnp.broadcast_to(jnp.exp(float(C) * lgf), (8, DV))

    @pl.when(i == 0)
    def _():
        S_ref[...] = jnp.zeros_like(S_ref)

    h = h_ref[0]
    q = _dot(h, win_ref[:, OFF_Q:OFF_K])

    hext_ref[0:HALO, :] = jnp.where(i > 0, hp_ref[0], jnp.zeros_like(hp_ref[0]))
    hext_ref[HALO:HALO + T, :] = h
    hext_ref[HALO + T:, :] = jnp.where(i < nt - 1, hn_ref[0], jnp.zeros_like(hn_ref[0]))
    xa_ref[...] = _dot(hext_ref[...], win_ref[:, 0:POOL_WIDTH])

    rot = [_rotary_chunk(bt_ref, ot_ref, ci) for ci in range(len(chunks))]
    qh = [[_rot(q[rows, hs], *rot[ci]) for hs in heads] for ci, rows in enumerate(chunks)]
    sc = [[_dot_nt(qh[ci][hd].astype(BF16), k_ref[0, rows, heads[hd]]) for hd in range(HEADS)]
          for ci, rows in enumerate(chunks)]

    xg = _dot(h, win_ref[:, OFF_G:OFF_GATE])

    ys = []
    for ci, rows in enumerate(chunks):
        ys.append([])
        for hd in range(HEADS):
            vb = v_ref[0, rows, vheads[hd]]
            S = S_ref[hd]
            qdec = (jnp.concatenate([qh[ci][hd], qh[ci][hd]], axis=-1) * qd_ref[hd]).astype(BF16)
            states = jnp.concatenate([S.astype(BF16), sb_ref[0, ci, hd]], axis=0)
            ys[ci].append(_dot((sc[ci][hd] * mask_ref[hd]).astype(BF16), vb) + _dot(qdec, states))
            kf = (k_ref[0, rows, heads[hd]].astype(F32) * kdf_ref[hd]).astype(BF16)
            S_ref[hd] = S * cdf_ref[hd][0:1, :] + _dot_tn(kf, vb)

    gz = _dot(h, win_ref[:, OFF_GATE:])

    tglob = i * T + lax.broadcasted_iota(jnp.int32, (T, POOL_GROUP_DIM), 0)
    pooled = []
    for gi, w in enumerate(POOL_WINDOWS):
        cols = slice(gi * POOL_GROUP_DIM, (gi + 1) * POOL_GROUP_DIM)
        acc = xa_ref[HALO - w // 2:HALO - w // 2 + T, cols]
        for kk in range(-w // 2 + 1, w // 2):
            acc = acc + xa_ref[HALO + kk:HALO + kk + T, cols]
        lo = jnp.maximum(tglob - w // 2, 0)
        hi = jnp.minimum(tglob + w // 2, seq_len)
        mean = acc / (hi - lo).astype(F32)
        d = mean - xa_ref[HALO:HALO + T, cols]
        pooled.append(_dot(d.astype(BF16), pw_ref[gi]))
    pooled = jnp.concatenate(pooled, axis=-1) * ps_ref[...]
    a = _dot(pooled.astype(BF16), wpo_ref[...])

    rs = []
    for ci, rows in enumerate(chunks):
        rin = []
        for hd in range(HEADS):
            y = ys[ci][hd]
            yc = y - jnp.mean(y, axis=-1, keepdims=True)
            yn = yc * lax.rsqrt(jnp.mean(yc * yc, axis=-1, keepdims=True) + GN_EPS)
            gh = xg[rows, vheads[hd]]
            rin.append((yn * (gh * jax.nn.sigmoid(gh))).astype(BF16))
        rs.append(_dot(jnp.concatenate(rin, axis=-1), wro_ref[...]))

    for rows, r in zip(chunks, rs):
        gates = jax.nn.sigmoid(gz[rows])
        merged = gates[:, :D_MODEL] * a[rows] + gates[:, D_MODEL:] * r
        o_ref[0, rows, :] = x_ref[0, rows, :] + _dot(merged.astype(BF16), wo_ref[...])


def _ffn_kernel(x_ref, p_ref, gf_ref, wfi_ref, wfo_ref, gp_ref, wpg_ref, wpp_ref, gfin_ref, o_ref,
                *, final_norm):
    rpg = x_ref.shape[1] // FFN_GROUPS
    groups = [slice(gi * rpg, (gi + 1) * rpg) for gi in range(FFN_GROUPS)]
    xs = [x_ref[0, rows, :] for rows in groups]
    gt, up, pp = [], [], []
    for x, rows in zip(xs, groups):
        h2 = _rms(x, gf_ref[...]).astype(BF16)
        gt.append(_dot(h2, wfi_ref[:, :D_FF]))
        up.append(_dot(h2, wfi_ref[:, D_FF:]))
        pp.append(_dot(p_ref[0, rows, :].astype(BF16), wpp_ref[...]))
    xs = [x + _dot((g * jax.nn.sigmoid(g) * u).astype(BF16), wfo_ref[...])
          for x, g, u in zip(xs, gt, up)]
    pg = [jax.nn.sigmoid(_dot(_rms(x, gp_ref[...]).astype(BF16), wpg_ref[...])) for x in xs]
    for x, rows, proj, gate in zip(xs, groups, pp, pg):
        x = x + proj * gate
        if final_norm:
            x = _rms(x, gfin_ref[...])
        o_ref[0, rows, :] = x


def _layer_spec(arr, layer):
    nd = arr.ndim - 1
    return pl.BlockSpec((None,) + arr.shape[1:], lambda b, i: (layer,) + (0,) * nd,
                        pipeline_mode=pl.Buffered(1))


def _const_spec(arr):
    nd = arr.ndim
    return pl.BlockSpec(arr.shape, lambda b, i: (0,) * nd, pipeline_mode=pl.Buffered(1))


def _params(semantics):
    return pltpu.CompilerParams(dimension_semantics=semantics, vmem_limit_bytes=VMEM_LIMIT)


def _kv(x, layer, g_mix, w_in, rot_base, rot_off, dl):
    B, L, D = x.shape
    T = T_KV
    nt = L // T
    cpt = T // CHUNK
    rev = lambda b, i: (b, nt - 1 - i, 0)
    return pl.pallas_call(
        _kv_kernel,
        grid=(B, nt),
        in_specs=[
            pl.BlockSpec((1, T, D), rev),
            _layer_spec(g_mix, layer),
            _layer_spec(w_in, layer),
            pl.BlockSpec((cpt, 3, DK), lambda b, i: (nt - 1 - i, 0, 0)),
            _const_spec(rot_off),
            _layer_spec(dl, layer),
        ],
        out_specs=[
            pl.BlockSpec((1, T, D), rev),
            pl.BlockSpec((1, T, QK_WIDTH), rev),
            pl.BlockSpec((1, T, V_WIDTH), rev),
            pl.BlockSpec((1, cpt, HEADS, DK, DV), lambda b, i: (b, nt - 1 - i, 0, 0, 0)),
        ],
        out_shape=[
            jax.ShapeDtypeStruct((B, L, D), BF16),
            jax.ShapeDtypeStruct((B, L, QK_WIDTH), BF16),
            jax.ShapeDtypeStruct((B, L, V_WIDTH), BF16),
            jax.ShapeDtypeStruct((B, L // CHUNK, HEADS, DK, DV), BF16),
        ],
        scratch_shapes=[
            pltpu.VMEM((HEADS, DK, DV), F32),
            pltpu.VMEM((HEADS, CHUNK, DK), F32),
            pltpu.VMEM((HEADS, 8, DV), F32),
        ],
        compiler_params=_params(("arbitrary", "arbitrary")),
        name="kv",
    )(x, g_mix, w_in, rot_base, rot_off, dl)


def _mixer(x, h, k, v, sb, layer, w_in, pool_w, pool_scale, w_pool_out, w_ret_out, w_o,
           rot_base, rot_off, dl):
    B, L, D = x.shape
    T = T_MIX
    nt = L // T
    cpt = T // CHUNK
    hb = T // HALO
    nhb = L // HALO
    tile = lambda b, i: (b, i, 0)
    return pl.pallas_call(
        functools.partial(_mixer_kernel, seq_len=L),
        grid=(B, nt),
        in_specs=[
            pl.BlockSpec((1, T, D), tile),
            pl.BlockSpec((1, T, D), tile),
            pl.BlockSpec((1, HALO, D), lambda b, i: (b, jnp.maximum(i * hb - 1, 0), 0)),
            pl.BlockSpec((1, HALO, D), lambda b, i: (b, jnp.minimum((i + 1) * hb, nhb - 1), 0)),
            pl.BlockSpec((1, T, QK_WIDTH), tile),
            pl.BlockSpec((1, T, V_WIDTH), tile),
            pl.BlockSpec((1, cpt, HEADS, DK, DV), lambda b, i: (b, i, 0, 0, 0)),
            pl.BlockSpec((cpt, 3, DK), lambda b, i: (i, 0, 0)),
            _const_spec(rot_off),
            _layer_spec(dl, layer),
            _layer_spec(w_in, layer),
            _layer_spec(pool_w, layer),
            _layer_spec(pool_scale, layer),
            _layer_spec(w_pool_out, layer),
            _layer_spec(w_ret_out, layer),
            _layer_spec(w_o, layer),
        ],
        out_specs=pl.BlockSpec((1, T, D), tile),
        out_shape=jax.ShapeDtypeStruct((B, L, D), F32),
        scratch_shapes=[
            pltpu.VMEM((HEADS, DK, DV), F32),
            pltpu.VMEM((HEADS, CHUNK, CHUNK), F32),
            pltpu.VMEM((HEADS, CHUNK, 2 * DK), F32),
            pltpu.VMEM((HEADS, CHUNK, DK), F32),
            pltpu.VMEM((HEADS, 8, DV), F32),
            pltpu.VMEM((T + 2 * HALO, D), BF16),
            pltpu.VMEM((T + 2 * HALO, POOL_WIDTH), F32),
        ],
        compiler_params=_params(("arbitrary", "arbitrary")),
        name="mixer",
    )(x, h, h, h, k, v, sb, rot_base, rot_off, dl, w_in, pool_w, pool_scale, w_pool_out, w_ret_out,
      w_o)


def _ffn(x, p, layer, g_ffn, w_ffn_in, w_ffn_out, g_ple, w_ple_gate, w_ple_proj, g_final, final_norm):
    B, L, D = x.shape
    T = T_FFN
    tile = lambda b, i: (b, i, 0)
    return pl.pallas_call(
        functools.partial(_ffn_kernel, final_norm=final_norm),
        grid=(B, L // T),
        in_specs=[
            pl.BlockSpec((1, T, D), tile),
            pl.BlockSpec((None, 1, T, PLE_DIM), lambda b, i: (layer, b, i, 0)),
            _layer_spec(g_ffn, layer),
            _layer_spec(w_ffn_in, layer),
            _layer_spec(w_ffn_out, layer),
            _layer_spec(g_ple, layer),
            _layer_spec(w_ple_gate, layer),
            _layer_spec(w_ple_proj, layer),
            _const_spec(g_final),
        ],
        out_specs=pl.BlockSpec((1, T, D), tile),
        out_shape=jax.ShapeDtypeStruct((B, L, D), F32),
        compiler_params=_params(("parallel", "parallel")),
        name="ffn",
    )(x, p, g_ffn, w_ffn_in, w_ffn_out, g_ple, w_ple_gate, w_ple_proj, g_final)


def _rotary_tables(L):
    half = DK // 2
    inv = 1.0 / (ROPE_BASE ** (jnp.arange(half, dtype=F32) / half))
    base = (jnp.arange(L // CHUNK) * CHUNK).astype(F32)[:, None] * inv
    off = jnp.arange(CHUNK).astype(F32)[:, None] * inv
    planes = lambda c, s: jnp.stack([jnp.concatenate([c, c], -1), jnp.concatenate([s, s], -1),
                                     jnp.concatenate([-s, s], -1)])
    rot_base = jnp.transpose(planes(jnp.cos(base), jnp.sin(base)), (1, 0, 2))
    return rot_base, planes(jnp.cos(off), jnp.sin(off))


def kernel(x_prompt, x_sample, p_prompt, p_sample, g_mix, w_in, pool_w, pool_scale, decay_logit,
           w_pool_out, w_ret_out, w_o, g_ffn, w_ffn_in, w_ffn_out, g_ple, w_ple_gate, w_ple_proj,
           g_final):
    depth = w_in.shape[0]
    L = x_prompt.shape[1]
    rot_base, rot_off = _rotary_tables(L)
    w_in_b = w_in.astype(BF16)
    pool_w_b = pool_w.astype(BF16)
    w_pool_out_b = w_pool_out.astype(BF16)
    w_ret_out_b = w_ret_out.astype(BF16)
    w_o_b = w_o.astype(BF16)
    w_ffn_in_b = w_ffn_in.astype(BF16)
    w_ffn_out_b = w_ffn_out.astype(BF16)
    w_ple_gate_b = w_ple_gate.astype(BF16)
    w_ple_proj_b = w_ple_proj.astype(BF16)
    dl = jnp.broadcast_to(decay_logit.astype(F32)[..., None, None], decay_logit.shape + (1, DV))
    rows = lambda v: v.reshape(v.shape[0], 1, v.shape[1])
    g_mix_r, g_ffn_r, g_ple_r, pool_scale_r = rows(g_mix), rows(g_ffn), rows(g_ple), rows(pool_scale)
    g_final_r = g_final.reshape(1, -1)

    def trunk(x, p):
        for l in range(depth):
            h, k, v, sb = _kv(x, l, g_mix_r, w_in_b, rot_base, rot_off, dl)
            x = _mixer(x, h, k, v, sb, l, w_in_b, pool_w_b, pool_scale_r, w_pool_out_b,
                       w_ret_out_b, w_o_b, rot_base, rot_off, dl)
            x = _ffn(x, p, l, g_ffn_r, w_ffn_in_b, w_ffn_out_b, g_ple_r, w_ple_gate_b,
                     w_ple_proj_b, g_final_r, l == depth - 1)
        return x

    return trunk(x_prompt, p_prompt), trunk(x_sample, p_sample)
```

```python
import functools

import jax
import jax.numpy as jnp
from jax import lax
from jax.experimental import pallas as pl
from jax.experimental.pallas import tpu as pltpu

F32 = jnp.float32
BF16 = jnp.bfloat16

D_MODEL = 1024
PLE_DIM = 256
POOL_GROUP_DIM = 128
POOL_WIDTH = 512
POOL_WINDOWS = (2, 4, 8, 16)
HEADS = 4
DK = 128
DV = 256
QK_WIDTH = HEADS * DK
V_WIDTH = HEADS * DV
D_FF = 2816
ROPE_BASE = 10000.0
RMS_EPS = 1e-6
GN_EPS = 1e-5

OFF_Q = POOL_WIDTH
OFF_K = OFF_Q + QK_WIDTH
OFF_V = OFF_K + QK_WIDTH
OFF_G = OFF_V + V_WIDTH
OFF_GATE = OFF_G + V_WIDTH

HALO = 16
CHUNK = 256
T_KV = 1024
T_MIX = 512
T_FFN = 1024
FFN_GROUPS = 4
VMEM_LIMIT = 56 * 1024 * 1024
FWD, BWD = 0, 1


def _rms(x, g):
    return x * lax.rsqrt(jnp.mean(x * x, axis=-1, keepdims=True) + RMS_EPS) * g


def _rot(x, c, s):
    return x * c + pltpu.roll(x, DK // 2, 1) * s


def _rotary_chunk(bt_ref, ot_ref, ci):
    cb, sb, sbs = bt_ref[ci, 0:1, :], bt_ref[ci, 1:2, :], bt_ref[ci, 2:3, :]
    co, so, sos = ot_ref[0], ot_ref[1], ot_ref[2]
    return cb * co - sb * so, sbs * co + cb * sos


def _dot(a, b):
    return jnp.dot(a, b, preferred_element_type=F32)


def _dot_tn(a, b):
    return lax.dot_general(a, b, (((0,), (0,)), ((), ())), preferred_element_type=F32)


def _dot_nt(a, b):
    return lax.dot_general(a, b, (((1,), (1,)), ((), ())), preferred_element_type=F32)


def _log_decay(dl_ref, direction, h):
    x = dl_ref[direction, h]
    return jnp.minimum(x, 0.0) - jnp.log1p(jnp.exp(-jnp.abs(x)))


def _kv_kernel(*refs, normalize):
    if normalize:
        (x_ref, g_ref, wk_ref, wv0_ref, wv1_ref, bt_ref, ot_ref, dl_ref,
         h_ref, k_ref, v_ref, sb_ref, S_ref, kdb_ref, cdb_ref) = refs
    else:
        (x_ref, wk_ref, wv0_ref, wv1_ref, bt_ref, ot_ref, dl_ref,
         k_ref, v_ref, sb_ref, S_ref, kdb_ref, cdb_ref) = refs
    b = pl.program_id(0)
    i = pl.program_id(1)

    @pl.when((b == 0) & (i == 0))
    def _():
        r = lax.broadcasted_iota(jnp.int32, (CHUNK, DK), 0).astype(F32)
        for hd in range(HEADS):
            lgb = _log_decay(dl_ref, BWD, hd)
            kdb_ref[hd] = jnp.exp(r * lgb[:, :DK]) * DK ** -0.5
            cdb_ref[hd] = jnp.broadcast_to(jnp.exp(float(CHUNK) * lgb), (8, DV))

    @pl.when(i == 0)
    def _():
        S_ref[...] = jnp.zeros_like(S_ref)

    chunks = [slice(ci * CHUNK, (ci + 1) * CHUNK) for ci in reversed(range(x_ref.shape[1] // CHUNK))]
    ks, vs = [], []
    for rows in chunks:
        if normalize:
            h = _rms(x_ref[0, rows, :], g_ref[...]).astype(BF16)
            h_ref[0, rows, :] = h
        else:
            h = x_ref[0, rows, :]
        ks.append(_dot(h, wk_ref[...]))
        v = jnp.concatenate([_dot(h, wv0_ref[...]), _dot(h, wv1_ref[...])], axis=-1).astype(BF16)
        v_ref[0, rows, :] = v
        vs.append(v)
    for rows, k, v in zip(chunks, ks, vs):
        ci = rows.start // CHUNK
        c, s = _rotary_chunk(bt_ref, ot_ref, ci)
        for hd in range(HEADS):
            kh = _rot(k[:, hd * DK:(hd + 1) * DK], c, s)
            k_ref[0, rows, hd * DK:(hd + 1) * DK] = kh.astype(BF16)
            S = S_ref[hd]
            sb_ref[0, ci, hd] = S.astype(BF16)
            upd = _dot_tn((kh * kdb_ref[hd]).astype(BF16), v[:, hd * DV:(hd + 1) * DV])
            S_ref[hd] = S * cdb_ref[hd][0:1, :] + upd


def _mixer_kernel(x_ref, h_ref, hp_ref, hn_ref, k_ref, v_ref, sb_ref, bt_ref, ot_ref, dl_ref,
                  win_ref, pw_ref, ps_ref, wpo_ref, wro_ref, wo_ref, o_ref,
                  S_ref, mask_ref, qd_ref, kdf_ref, cdf_ref, hext_ref, xa_ref, wpool_ref, *, seq_len):
    b = pl.program_id(0)
    i = pl.program_id(1)
    nt = pl.num_programs(1)
    T = x_ref.shape[1]
    C = CHUNK
    chunks = [slice(ci * C, (ci + 1) * C) for ci in range(T // C)]
    heads = [slice(hd * DK, (hd + 1) * DK) for hd in range(HEADS)]
    vheads = [slice(hd * DV, (hd + 1) * DV) for hd in range(HEADS)]

    @pl.when((b == 0) & (i == 0))
    def _():
        ri = lax.broadcasted_iota(jnp.int32, (C, C), 0).astype(F32)
        ci = lax.broadcasted_iota(jnp.int32, (C, C), 1).astype(F32)
        r = lax.broadcasted_iota(jnp.int32, (C, DK), 0).astype(F32)
        for hd in range(HEADS):
            lgf = _log_decay(dl_ref, FWD, hd)
            lgb = _log_decay(dl_ref, BWD, hd)
            dfw = ri - ci
            m = jnp.where(dfw >= 0.0,
                          jnp.exp(lgf[:, :C] * jnp.maximum(dfw, 0.0)),
                          jnp.exp(lgb[:, :C] * jnp.maximum(-dfw, 0.0)))
            mask_ref[hd] = m * DK ** -0.5
            qd_ref[hd, :, 0:DK] = jnp.exp((r + 1.0) * lgf[:, :DK])
            qd_ref[hd, :, DK:2 * DK] = jnp.exp((C - r) * lgb[:, :DK])
            kdf_ref[hd] = jnp.exp((C - 1.0 - r) * lgf[:, :DK]) * DK ** -0.5
            cdf_ref[hd] = jnp.broadcast_to(jnp.exp(float(C) * lgf), (8, DV))
        for gi in range(len(POOL_WINDOWS)):
            cols = slice(gi * POOL_GROUP_DIM, (gi + 1) * POOL_GROUP_DIM)
            wpool_ref[cols, :] = jnp.dot(pw_ref[gi] * ps_ref[:, cols], wpo_ref[cols, :],
                                         precision=lax.Precision.HIGHEST,
                                         preferred_element_type=F32).astype(BF16)

    @pl.when(i == 0)
    def _():
        S_ref[...] = jnp.zeros_like(S_ref)

    h = h_ref[0]
    q = _dot(h, win_ref[:, OFF_Q:OFF_K])

    hext_ref[0:HALO, :] = jnp.where(i > 0, hp_ref[0], jnp.zeros_like(hp_ref[0]))
    hext_ref[HALO:HALO + T, :] = h
    hext_ref[HALO + T:, :] = jnp.where(i < nt - 1, hn_ref[0], jnp.zeros_like(hn_ref[0]))
    xa_ref[...] = _dot(hext_ref[...], win_ref[:, 0:POOL_WIDTH])

    rot = [_rotary_chunk(bt_ref, ot_ref, ci) for ci in range(len(chunks))]
    qh = [[_rot(q[rows, hs], *rot[ci]) for hs in heads] for ci, rows in enumerate(chunks)]
    sc = [[_dot_nt(qh[ci][hd].astype(BF16), k_ref[0, rows, heads[hd]]) for hd in range(HEADS)]
          for ci, rows in enumerate(chunks)]

    xg = _dot(h, win_ref[:, OFF_G:OFF_GATE])

    ys = []
    for ci, rows in enumerate(chunks):
        ys.append([])
        for hd in range(HEADS):
            vb = v_ref[0, rows, vheads[hd]]
            S = S_ref[hd]
            qdec = (jnp.concatenate([qh[ci][hd], qh[ci][hd]], axis=-1) * qd_ref[hd]).astype(BF16)
            states = jnp.concatenate([S.astype(BF16), sb_ref[0, ci, hd]], axis=0)
            ys[ci].append(_dot((sc[ci][hd] * mask_ref[hd]).astype(BF16), vb) + _dot(qdec, states))
            kf = (k_ref[0, rows, heads[hd]].astype(F32) * kdf_ref[hd]).astype(BF16)
            S_ref[hd] = S * cdf_ref[hd][0:1, :] + _dot_tn(kf, vb)

    gz = _dot(h, win_ref[:, OFF_GATE:])

    tglob = i * T + lax.broadcasted_iota(jnp.int32, (T, POOL_GROUP_DIM), 0)
    diffs = []
    for gi, w in enumerate(POOL_WINDOWS):
        cols = slice(gi * POOL_GROUP_DIM, (gi + 1) * POOL_GROUP_DIM)
        acc = xa_ref[HALO - w // 2:HALO - w // 2 + T, cols]
        for kk in range(-w // 2 + 1, w // 2):
            acc = acc + xa_ref[HALO + kk:HALO + kk + T, cols]
        lo = jnp.maximum(tglob - w // 2, 0)
        hi = jnp.minimum(tglob + w // 2, seq_len)
        mean = acc / (hi - lo).astype(F32)
        diffs.append((mean - xa_ref[HALO:HALO + T, cols]).astype(BF16))
    a = _dot(jnp.concatenate(diffs, axis=-1), wpool_ref[...])

    rs = []
    for ci, rows in enumerate(chunks):
        rin = []
        for hd in range(HEADS):
            y = ys[ci][hd]
            yc = y - jnp.mean(y, axis=-1, keepdims=True)
            yn = yc * lax.rsqrt(jnp.mean(yc * yc, axis=-1, keepdims=True) + GN_EPS)
            gh = xg[rows, vheads[hd]]
            rin.append((yn * (gh * jax.nn.sigmoid(gh))).astype(BF16))
        rs.append(_dot(jnp.concatenate(rin, axis=-1), wro_ref[...]))

    for rows, r in zip(chunks, rs):
        gates = jax.nn.sigmoid(gz[rows])
        merged = gates[:, :D_MODEL] * a[rows] + gates[:, D_MODEL:] * r
        o_ref[0, rows, :] = x_ref[0, rows, :] + _dot(merged.astype(BF16), wo_ref[...])


def _ffn_kernel(x_ref, p_ref, gf_ref, wfi_ref, wfo_ref, gp_ref, wpg_ref, wpp_ref, gout_ref, o_ref,
                *hn_ref, final_norm):
    rpg = x_ref.shape[1] // FFN_GROUPS
    groups = [slice(gi * rpg, (gi + 1) * rpg) for gi in range(FFN_GROUPS)]
    xs = [x_ref[0, rows, :] for rows in groups]
    gt, up, pp = [], [], []
    for x, rows in zip(xs, groups):
        h2 = _rms(x, gf_ref[...]).astype(BF16)
        gt.append(_dot(h2, wfi_ref[:, :D_FF]))
        up.append(_dot(h2, wfi_ref[:, D_FF:]))
        pp.append(_dot(p_ref[0, rows, :].astype(BF16), wpp_ref[...]))
    xs = [x + _dot((g * jax.nn.sigmoid(g) * u).astype(BF16), wfo_ref[...])
          for x, g, u in zip(xs, gt, up)]
    pg = [jax.nn.sigmoid(_dot(_rms(x, gp_ref[...]).astype(BF16), wpg_ref[...])) for x in xs]
    for x, rows, proj, gate in zip(xs, groups, pp, pg):
        x = x + proj * gate
        if final_norm:
            o_ref[0, rows, :] = _rms(x, gout_ref[...])
        else:
            o_ref[0, rows, :] = x
            hn_ref[0][0, rows, :] = _rms(x, gout_ref[...]).astype(BF16)


def _layer_spec(arr, layer):
    nd = arr.ndim - 1
    return pl.BlockSpec((None,) + arr.shape[1:], lambda b, i: (layer,) + (0,) * nd,
                        pipeline_mode=pl.Buffered(1))


def _const_spec(arr):
    nd = arr.ndim
    return pl.BlockSpec(arr.shape, lambda b, i: (0,) * nd, pipeline_mode=pl.Buffered(1))


def _params(semantics):
    return pltpu.CompilerParams(dimension_semantics=semantics, vmem_limit_bytes=VMEM_LIMIT)


def _kv(x, layer, g_mix, w_in, rot_base, rot_off, dl, normalize):
    B, L, D = x.shape
    T = T_KV
    nt = L // T
    cpt = T // CHUNK
    rev = lambda b, i: (b, nt - 1 - i, 0)
    wcols = lambda off: pl.BlockSpec((None, D, QK_WIDTH), lambda b, i: (layer, 0, off // QK_WIDTH),
                                     pipeline_mode=pl.Buffered(1))
    in_specs = [pl.BlockSpec((1, T, D), rev)]
    args = [x]
    out_specs, out_shape = [], []
    if normalize:
        in_specs.append(_layer_spec(g_mix, layer))
        args.append(g_mix)
        out_specs.append(pl.BlockSpec((1, T, D), rev))
        out_shape.append(jax.ShapeDtypeStruct((B, L, D), BF16))
    in_specs += [
        wcols(OFF_K),
        wcols(OFF_V),
        wcols(OFF_V + QK_WIDTH),
        pl.BlockSpec((cpt, 3, DK), lambda b, i: (nt - 1 - i, 0, 0)),
        _const_spec(rot_off),
        _layer_spec(dl, layer),
    ]
    args += [w_in, w_in, w_in, rot_base, rot_off, dl]
    out_specs += [
        pl.BlockSpec((1, T, QK_WIDTH), rev),
        pl.BlockSpec((1, T, V_WIDTH), rev),
        pl.BlockSpec((1, cpt, HEADS, DK, DV), lambda b, i: (b, nt - 1 - i, 0, 0, 0)),
    ]
    out_shape += [
        jax.ShapeDtypeStruct((B, L, QK_WIDTH), BF16),
        jax.ShapeDtypeStruct((B, L, V_WIDTH), BF16),
        jax.ShapeDtypeStruct((B, L // CHUNK, HEADS, DK, DV), BF16),
    ]
    return pl.pallas_call(
        functools.partial(_kv_kernel, normalize=normalize),
        grid=(B, nt),
        in_specs=in_specs,
        out_specs=out_specs,
        out_shape=out_shape,
        scratch_shapes=[
            pltpu.VMEM((HEADS, DK, DV), F32),
            pltpu.VMEM((HEADS, CHUNK, DK), F32),
            pltpu.VMEM((HEADS, 8, DV), F32),
        ],
        compiler_params=_params(("arbitrary", "arbitrary")),
        name="kv",
    )(*args)


def _mixer(x, h, k, v, sb, layer, w_in, pool_w, pool_scale, w_pool_out, w_ret_out, w_o,
           rot_base, rot_off, dl):
    B, L, D = x.shape
    T = T_MIX
    nt = L // T
    cpt = T // CHUNK
    hb = T // HALO
    nhb = L // HALO
    tile = lambda b, i: (b, i, 0)
    return pl.pallas_call(
        functools.partial(_mixer_kernel, seq_len=L),
        grid=(B, nt),
        in_specs=[
            pl.BlockSpec((1, T, D), tile),
            pl.BlockSpec((1, T, D), tile),
            pl.BlockSpec((1, HALO, D), lambda b, i: (b, jnp.maximum(i * hb - 1, 0), 0)),
            pl.BlockSpec((1, HALO, D), lambda b, i: (b, jnp.minimum((i + 1) * hb, nhb - 1), 0)),
            pl.BlockSpec((1, T, QK_WIDTH), tile),
            pl.BlockSpec((1, T, V_WIDTH), tile),
            pl.BlockSpec((1, cpt, HEADS, DK, DV), lambda b, i: (b, i, 0, 0, 0)),
            pl.BlockSpec((cpt, 3, DK), lambda b, i: (i, 0, 0)),
            _const_spec(rot_off),
            _layer_spec(dl, layer),
            _layer_spec(w_in, layer),
            _layer_spec(pool_w, layer),
            _layer_spec(pool_scale, layer),
            _layer_spec(w_pool_out, layer),
            _layer_spec(w_ret_out, layer),
            _layer_spec(w_o, layer),
        ],
        out_specs=pl.BlockSpec((1, T, D), tile),
        out_shape=jax.ShapeDtypeStruct((B, L, D), F32),
        scratch_shapes=[
            pltpu.VMEM((HEADS, DK, DV), F32),
            pltpu.VMEM((HEADS, CHUNK, CHUNK), F32),
            pltpu.VMEM((HEADS, CHUNK, 2 * DK), F32),
            pltpu.VMEM((HEADS, CHUNK, DK), F32),
            pltpu.VMEM((HEADS, 8, DV), F32),
            pltpu.VMEM((T + 2 * HALO, D), BF16),
            pltpu.VMEM((T + 2 * HALO, POOL_WIDTH), F32),
            pltpu.VMEM((POOL_WIDTH, D), BF16),
        ],
        compiler_params=_params(("arbitrary", "arbitrary")),
        name="mixer",
    )(x, h, h, h, k, v, sb, rot_base, rot_off, dl, w_in, pool_w, pool_scale, w_pool_out, w_ret_out,
      w_o)


def _ffn(x, p, layer, g_ffn, w_ffn_in, w_ffn_out, g_ple, w_ple_gate, w_ple_proj, g_out, final_norm):
    B, L, D = x.shape
    T = T_FFN
    tile = lambda b, i: (b, i, 0)
    out_specs = [pl.BlockSpec((1, T, D), tile)]
    out_shape = [jax.ShapeDtypeStruct((B, L, D), F32)]
    if not final_norm:
        out_specs.append(pl.BlockSpec((1, T, D), tile))
        out_shape.append(jax.ShapeDtypeStruct((B, L, D), BF16))
    return pl.pallas_call(
        functools.partial(_ffn_kernel, final_norm=final_norm),
        grid=(B, L // T),
        in_specs=[
            pl.BlockSpec((1, T, D), tile),
            pl.BlockSpec((None, 1, T, PLE_DIM), lambda b, i: (layer, b, i, 0)),
            _layer_spec(g_ffn, layer),
            _layer_spec(w_ffn_in, layer),
            _layer_spec(w_ffn_out, layer),
            _layer_spec(g_ple, layer),
            _layer_spec(w_ple_gate, layer),
            _layer_spec(w_ple_proj, layer),
            _const_spec(g_out),
        ],
        out_specs=out_specs,
        out_shape=out_shape,
        compiler_params=_params(("parallel", "parallel")),
        name="ffn",
    )(x, p, g_ffn, w_ffn_in, w_ffn_out, g_ple, w_ple_gate, w_ple_proj, g_out)


def _rotary_tables(L):
    half = DK // 2
    inv = 1.0 / (ROPE_BASE ** (jnp.arange(half, dtype=F32) / half))
    base = (jnp.arange(L // CHUNK) * CHUNK).astype(F32)[:, None] * inv
    off = jnp.arange(CHUNK).astype(F32)[:, None] * inv
    planes = lambda c, s: jnp.stack([jnp.concatenate([c, c], -1), jnp.concatenate([s, s], -1),
                                     jnp.concatenate([-s, s], -1)])
    rot_base = jnp.transpose(planes(jnp.cos(base), jnp.sin(base)), (1, 0, 2))
    return rot_base, planes(jnp.cos(off), jnp.sin(off))


def kernel(x_prompt, x_sample, p_prompt, p_sample, g_mix, w_in, pool_w, pool_scale, decay_logit,
           w_pool_out, w_ret_out, w_o, g_ffn, w_ffn_in, w_ffn_out, g_ple, w_ple_gate, w_ple_proj,
           g_final):
    depth = w_in.shape[0]
    L = x_prompt.shape[1]
    rot_base, rot_off = _rotary_tables(L)
    w_in_b = w_in.astype(BF16)
    w_ret_out_b = w_ret_out.astype(BF16)
    w_o_b = w_o.astype(BF16)
    w_ffn_in_b = w_ffn_in.astype(BF16)
    w_ffn_out_b = w_ffn_out.astype(BF16)
    w_ple_gate_b = w_ple_gate.astype(BF16)
    w_ple_proj_b = w_ple_proj.astype(BF16)
    dl = jnp.broadcast_to(decay_logit.astype(F32)[..., None, None], decay_logit.shape + (1, DV))
    rows = lambda v: v.reshape(v.shape[0], 1, v.shape[1])
    g_mix_r, g_ffn_r, g_ple_r, pool_scale_r = rows(g_mix), rows(g_ffn), rows(g_ple), rows(pool_scale)
    g_final_r = g_final.reshape(1, -1)

    def trunk(x, p):
        h = None
        for l in range(depth):
            if h is None:
                h, k, v, sb = _kv(x, l, g_mix_r, w_in_b, rot_base, rot_off, dl, True)
            else:
                k, v, sb = _kv(h, l, g_mix_r, w_in_b, rot_base, rot_off, dl, False)
            x = _mixer(x, h, k, v, sb, l, w_in_b, pool_w, pool_scale_r, w_pool_out,
                       w_ret_out_b, w_o_b, rot_base, rot_off, dl)
            if l == depth - 1:
                x, = _ffn(x, p, l, g_ffn_r, w_ffn_in_b, w_ffn_out_b, g_ple_r, w_ple_gate_b,
                          w_ple_proj_b, g_final_r, True)
            else:
                x, h = _ffn(x, p, l, g_ffn_r, w_ffn_in_b, w_ffn_out_b, g_ple_r, w_ple_gate_b,
                            w_ple_proj_b, g_mix_r[l + 1], False)
        return x

    return trunk(x_prompt, p_prompt), trunk(x_sample, p_sample)
```

```python
import functools

import jax
import jax.numpy as jnp
from jax import lax
from jax.experimental import pallas as pl
from jax.experimental.pallas import tpu as pltpu

F32 = jnp.float32
BF16 = jnp.bfloat16

D_MODEL = 1024
PLE_DIM = 256
POOL_GROUP_DIM = 128
POOL_WIDTH = 512
POOL_WINDOWS = (2, 4, 8, 16)
HEADS = 4
DK = 128
DV = 256
QK_WIDTH = HEADS * DK
V_WIDTH = HEADS * DV
D_FF = 2816
ROPE_BASE = 10000.0
RMS_EPS = 1e-6
GN_EPS = 1e-5

OFF_Q = POOL_WIDTH
OFF_K = OFF_Q + QK_WIDTH
OFF_V = OFF_K + QK_WIDTH
OFF_G = OFF_V + V_WIDTH
OFF_GATE = OFF_G + V_WIDTH

HALO = 16
CHUNK = 256
T_KV = 1024
T_MIX = 512
T_FFN = 1024
FFN_GROUPS = 4
VMEM_LIMIT = 60 * 1024 * 1024
FWD, BWD = 0, 1


def _rms(x, g):
    return x * lax.rsqrt(jnp.mean(x * x, axis=-1, keepdims=True) + RMS_EPS) * g


def _rot(x, c, s):
    return x * c + pltpu.roll(x, DK // 2, 1) * s


def _rotary_chunk(bt_ref, ot_ref, ci):
    cb, sb, sbs = bt_ref[ci, 0:1, :], bt_ref[ci, 1:2, :], bt_ref[ci, 2:3, :]
    co, so, sos = ot_ref[0], ot_ref[1], ot_ref[2]
    return cb * co - sb * so, sbs * co + cb * sos


def _dot(a, b):
    return jnp.dot(a, b, preferred_element_type=F32)


def _dot_tn(a, b):
    return lax.dot_general(a, b, (((0,), (0,)), ((), ())), preferred_element_type=F32)


def _dot_nt(a, b):
    return lax.dot_general(a, b, (((1,), (1,)), ((), ())), preferred_element_type=F32)


def _log_decay(dl_ref, direction, h):
    x = dl_ref[direction, h]
    return jnp.minimum(x, 0.0) - jnp.log1p(jnp.exp(-jnp.abs(x)))


def _kv_kernel(*refs, normalize):
    if normalize:
        (xp_ref, xs_ref, g_ref, wk_ref, wv0_ref, wv1_ref, bt_ref, ot_ref, dl_ref,
         h_ref, k_ref, v_ref, sb_ref, S_ref, kdb_ref, cdb_ref) = refs
    else:
        (x_ref, wk_ref, wv0_ref, wv1_ref, bt_ref, ot_ref, dl_ref,
         k_ref, v_ref, sb_ref, S_ref, kdb_ref, cdb_ref) = refs
    b = pl.program_id(0)
    i = pl.program_id(1)

    @pl.when((b == 0) & (i == 0))
    def _():
        r = lax.broadcasted_iota(jnp.int32, (CHUNK, DK), 0).astype(F32)
        for hd in range(HEADS):
            lgb = _log_decay(dl_ref, BWD, hd)
            kdb_ref[hd] = jnp.exp(r * lgb[:, :DK]) * DK ** -0.5
            cdb_ref[hd] = jnp.broadcast_to(jnp.exp(float(CHUNK) * lgb), (8, DV))

    @pl.when(i == 0)
    def _():
        S_ref[...] = jnp.zeros_like(S_ref)

    chunks = [slice(ci * CHUNK, (ci + 1) * CHUNK) for ci in reversed(range(k_ref.shape[1] // CHUNK))]
    ks, vs = [], []
    for rows in chunks:
        if normalize:
            x = jnp.where(b == 0, xp_ref[0, rows, :], xs_ref[0, rows, :])
            h = _rms(x, g_ref[...]).astype(BF16)
            h_ref[0, rows, :] = h
        else:
            h = x_ref[0, rows, :]
        ks.append(_dot(h, wk_ref[...]))
        v = jnp.concatenate([_dot(h, wv0_ref[...]), _dot(h, wv1_ref[...])], axis=-1).astype(BF16)
        v_ref[0, rows, :] = v
        vs.append(v)
    for rows, k, v in zip(chunks, ks, vs):
        ci = rows.start // CHUNK
        c, s = _rotary_chunk(bt_ref, ot_ref, ci)
        for hd in range(HEADS):
            kh = _rot(k[:, hd * DK:(hd + 1) * DK], c, s)
            k_ref[0, rows, hd * DK:(hd + 1) * DK] = kh.astype(BF16)
            S = S_ref[hd]
            sb_ref[0, ci, hd] = S.astype(BF16)
            upd = _dot_tn((kh * kdb_ref[hd]).astype(BF16), v[:, hd * DV:(hd + 1) * DV])
            S_ref[hd] = S * cdb_ref[hd][0:1, :] + upd


def _mixer_kernel(*refs, seq_len, paired):
    xs_ref = None
    if paired:
        x_ref, xs_ref, *refs = refs
    else:
        x_ref, *refs = refs
    (h_ref, hp_ref, hn_ref, k_ref, v_ref, sb_ref, bt_ref, ot_ref, dl_ref,
     win_ref, pw_ref, ps_ref, wpo_ref, wro_ref, wo_ref, o_ref,
     S_ref, mask_ref, qd_ref, kdf_ref, cdf_ref, hext_ref, xa_ref, wpool_ref) = refs
    b = pl.program_id(0)
    i = pl.program_id(1)
    nt = pl.num_programs(1)
    T = h_ref.shape[1]
    C = CHUNK
    chunks = [slice(ci * C, (ci + 1) * C) for ci in range(T // C)]
    heads = [slice(hd * DK, (hd + 1) * DK) for hd in range(HEADS)]
    vheads = [slice(hd * DV, (hd + 1) * DV) for hd in range(HEADS)]

    @pl.when((b == 0) & (i == 0))
    def _():
        ri = lax.broadcasted_iota(jnp.int32, (C, C), 0).astype(F32)
        ci = lax.broadcasted_iota(jnp.int32, (C, C), 1).astype(F32)
        r = lax.broadcasted_iota(jnp.int32, (C, DK), 0).astype(F32)
        for hd in range(HEADS):
            lgf = _log_decay(dl_ref, FWD, hd)
            lgb = _log_decay(dl_ref, BWD, hd)
            dfw = ri - ci
            m = jnp.where(dfw >= 0.0,
                          jnp.exp(lgf[:, :C] * jnp.maximum(dfw, 0.0)),
                          jnp.exp(lgb[:, :C] * jnp.maximum(-dfw, 0.0)))
            mask_ref[hd] = m * DK ** -0.5
            qd_ref[hd, :, 0:DK] = jnp.exp((r + 1.0) * lgf[:, :DK])
            qd_ref[hd, :, DK:2 * DK] = jnp.exp((C - r) * lgb[:, :DK])
            kdf_ref[hd] = jnp.exp((C - 1.0 - r) * lgf[:, :DK]) * DK ** -0.5
            cdf_ref[hd] = jnp.broadcast_to(jnp.exp(float(C) * lgf), (8, DV))
        for gi in range(len(POOL_WINDOWS)):
            cols = slice(gi * POOL_GROUP_DIM, (gi + 1) * POOL_GROUP_DIM)
            wpool_ref[cols, :] = jnp.dot(pw_ref[gi] * ps_ref[:, cols], wpo_ref[cols, :],
                                         precision=lax.Precision.HIGHEST,
                                         preferred_element_type=F32).astype(BF16)

    @pl.when(i == 0)
    def _():
        S_ref[...] = jnp.zeros_like(S_ref)

    h = h_ref[0]
    q = _dot(h, win_ref[:, OFF_Q:OFF_K])

    hext_ref[0:HALO, :] = jnp.where(i > 0, hp_ref[0], jnp.zeros_like(hp_ref[0]))
    hext_ref[HALO:HALO + T, :] = h
    hext_ref[HALO + T:, :] = jnp.where(i < nt - 1, hn_ref[0], jnp.zeros_like(hn_ref[0]))
    xa_ref[...] = _dot(hext_ref[...], win_ref[:, 0:POOL_WIDTH])

    rot = [_rotary_chunk(bt_ref, ot_ref, ci) for ci in range(len(chunks))]
    qh = [[_rot(q[rows, hs], *rot[ci]) for hs in heads] for ci, rows in enumerate(chunks)]
    sc = [[_dot_nt(qh[ci][hd].astype(BF16), k_ref[0, rows, heads[hd]]) for hd in range(HEADS)]
          for ci, rows in enumerate(chunks)]

    xg = _dot(h, win_ref[:, OFF_G:OFF_GATE])

    ys = []
    for ci, rows in enumerate(chunks):
        ys.append([])
        for hd in range(HEADS):
            vb = v_ref[0, rows, vheads[hd]]
            S = S_ref[hd]
            qdec = (jnp.concatenate([qh[ci][hd], qh[ci][hd]], axis=-1) * qd_ref[hd]).astype(BF16)
            states = jnp.concatenate([S.astype(BF16), sb_ref[0, ci, hd]], axis=0)
            ys[ci].append(_dot((sc[ci][hd] * mask_ref[hd]).astype(BF16), vb) + _dot(qdec, states))
            kf = (k_ref[0, rows, heads[hd]].astype(F32) * kdf_ref[hd]).astype(BF16)
            S_ref[hd] = S * cdf_ref[hd][0:1, :] + _dot_tn(kf, vb)

    gz = _dot(h, win_ref[:, OFF_GATE:])

    tglob = i * T + lax.broadcasted_iota(jnp.int32, (T, POOL_GROUP_DIM), 0)
    diffs = []
    for gi, w in enumerate(POOL_WINDOWS):
        cols = slice(gi * POOL_GROUP_DIM, (gi + 1) * POOL_GROUP_DIM)
        acc = xa_ref[HALO - w // 2:HALO - w // 2 + T, cols]
        for kk in range(-w // 2 + 1, w // 2):
            acc = acc + xa_ref[HALO + kk:HALO + kk + T, cols]
        lo = jnp.maximum(tglob - w // 2, 0)
        hi = jnp.minimum(tglob + w // 2, seq_len)
        mean = acc / (hi - lo).astype(F32)
        diffs.append((mean - xa_ref[HALO:HALO + T, cols]).astype(BF16))
    a = _dot(jnp.concatenate(diffs, axis=-1), wpool_ref[...])

    rs = []
    for ci, rows in enumerate(chunks):
        rin = []
        for hd in range(HEADS):
            y = ys[ci][hd]
            yc = y - jnp.mean(y, axis=-1, keepdims=True)
            yn = yc * lax.rsqrt(jnp.mean(yc * yc, axis=-1, keepdims=True) + GN_EPS)
            gh = xg[rows, vheads[hd]]
            rin.append((yn * (gh * jax.nn.sigmoid(gh))).astype(BF16))
        rs.append(_dot(jnp.concatenate(rin, axis=-1), wro_ref[...]))

    for rows, r in zip(chunks, rs):
        gates = jax.nn.sigmoid(gz[rows])
        merged = gates[:, :D_MODEL] * a[rows] + gates[:, D_MODEL:] * r
        x = x_ref[0, rows, :]
        if paired:
            x = jnp.where(b == 0, x, xs_ref[0, rows, :])
        o_ref[0, rows, :] = x + _dot(merged.astype(BF16), wo_ref[...])


def _ffn_kernel(*refs, final_norm):
    if final_norm:
        x_ref, p_ref, gf_ref, wfi_ref, wfo_ref, gp_ref, wpg_ref, wpp_ref, gout_ref, o_ref = refs
    else:
        (x_ref, p_ref, ps_ref, gf_ref, wfi_ref, wfo_ref, gp_ref, wpg_ref, wpp_ref, gout_ref,
         o_ref, hn_ref) = refs
    b = pl.program_id(0)
    rpg = x_ref.shape[1] // FFN_GROUPS
    groups = [slice(gi * rpg, (gi + 1) * rpg) for gi in range(FFN_GROUPS)]
    xs = [x_ref[0, rows, :] for rows in groups]
    gt, up, pp = [], [], []
    for x, rows in zip(xs, groups):
        h2 = _rms(x, gf_ref[...]).astype(BF16)
        gt.append(_dot(h2, wfi_ref[:, :D_FF]))
        up.append(_dot(h2, wfi_ref[:, D_FF:]))
        p = p_ref[0, rows, :]
        if not final_norm:
            p = jnp.where(b == 0, p, ps_ref[0, rows, :])
        pp.append(_dot(p.astype(BF16), wpp_ref[...]))
    xs = [x + _dot((g * jax.nn.sigmoid(g) * u).astype(BF16), wfo_ref[...])
          for x, g, u in zip(xs, gt, up)]
    pg = [jax.nn.sigmoid(_dot(_rms(x, gp_ref[...]).astype(BF16), wpg_ref[...])) for x in xs]
    for x, rows, proj, gate in zip(xs, groups, pp, pg):
        x = x + proj * gate
        if final_norm:
            o_ref[0, rows, :] = _rms(x, gout_ref[...])
        else:
            o_ref[0, rows, :] = x
            hn_ref[0, rows, :] = _rms(x, gout_ref[...]).astype(BF16)


def _layer_spec(arr, layer):
    nd = arr.ndim - 1
    return pl.BlockSpec((None,) + arr.shape[1:], lambda b, i: (layer,) + (0,) * nd,
                        pipeline_mode=pl.Buffered(1))


def _const_spec(arr):
    nd = arr.ndim
    return pl.BlockSpec(arr.shape, lambda b, i: (0,) * nd, pipeline_mode=pl.Buffered(1))


def _pair_specs(block, nt, tile_of, lead=()):
    first, last = tile_of(0), tile_of(nt - 1)
    prompt = pl.BlockSpec(block, lambda b, i: lead + (0, jnp.where(b == 0, tile_of(i), last), 0))
    sample = pl.BlockSpec(block, lambda b, i: lead + (jnp.maximum(b - 1, 0),
                                                      jnp.where(b == 0, first, tile_of(i)), 0))
    return [prompt, sample]


def _params(semantics):
    return pltpu.CompilerParams(dimension_semantics=semantics, vmem_limit_bytes=VMEM_LIMIT)


def _kv(x, layer, g_mix, w_in, rot_base, rot_off, dl, normalize):
    B = x[0].shape[0] + x[1].shape[0] if normalize else x.shape[0]
    _, L, D = x[0].shape if normalize else x.shape
    T = T_KV
    nt = L // T
    cpt = T // CHUNK
    rev = lambda b, i: (b, nt - 1 - i, 0)
    wcols = lambda off: pl.BlockSpec((None, D, QK_WIDTH), lambda b, i: (layer, 0, off // QK_WIDTH),
                                     pipeline_mode=pl.Buffered(1))
    out_specs, out_shape = [], []
    if normalize:
        in_specs = _pair_specs((1, T, D), nt, lambda i: nt - 1 - i) + [_layer_spec(g_mix, layer)]
        args = [x[0], x[1], g_mix]
        out_specs.append(pl.BlockSpec((1, T, D), rev))
        out_shape.append(jax.ShapeDtypeStruct((B, L, D), BF16))
    else:
        in_specs = [pl.BlockSpec((1, T, D), rev)]
        args = [x]
    in_specs += [
        wcols(OFF_K),
        wcols(OFF_V),
        wcols(OFF_V + QK_WIDTH),
        pl.BlockSpec((cpt, 3, DK), lambda b, i: (nt - 1 - i, 0, 0)),
        _const_spec(rot_off),
        _layer_spec(dl, layer),
    ]
    args += [w_in, w_in, w_in, rot_base, rot_off, dl]
    out_specs += [
        pl.BlockSpec((1, T, QK_WIDTH), rev),
        pl.BlockSpec((1, T, V_WIDTH), rev),
        pl.BlockSpec((1, cpt, HEADS, DK, DV), lambda b, i: (b, nt - 1 - i, 0, 0, 0)),
    ]
    out_shape += [
        jax.ShapeDtypeStruct((B, L, QK_WIDTH), BF16),
        jax.ShapeDtypeStruct((B, L, V_WIDTH), BF16),
        jax.ShapeDtypeStruct((B, L // CHUNK, HEADS, DK, DV), BF16),
    ]
    return pl.pallas_call(
        functools.partial(_kv_kernel, normalize=normalize),
        grid=(B, nt),
        in_specs=in_specs,
        out_specs=out_specs,
        out_shape=out_shape,
        scratch_shapes=[
            pltpu.VMEM((HEADS, DK, DV), F32),
            pltpu.VMEM((HEADS, CHUNK, DK), F32),
            pltpu.VMEM((HEADS, 8, DV), F32),
        ],
        compiler_params=_params(("arbitrary", "arbitrary")),
        name="kv",
    )(*args)


def _mixer(x, h, k, v, sb, layer, w_in, pool_w, pool_scale, w_pool_out, w_ret_out, w_o,
           rot_base, rot_off, dl):
    paired = isinstance(x, tuple)
    B, L, D = h.shape
    T = T_MIX
    nt = L // T
    cpt = T // CHUNK
    hb = T // HALO
    nhb = L // HALO
    tile = lambda b, i: (b, i, 0)
    return pl.pallas_call(
        functools.partial(_mixer_kernel, seq_len=L, paired=paired),
        grid=(B, nt),
        in_specs=(_pair_specs((1, T, D), nt, lambda i: i) if paired
                  else [pl.BlockSpec((1, T, D), tile)]) + [
            pl.BlockSpec((1, T, D), tile),
            pl.BlockSpec((1, HALO, D), lambda b, i: (b, jnp.maximum(i * hb - 1, 0), 0)),
            pl.BlockSpec((1, HALO, D), lambda b, i: (b, jnp.minimum((i + 1) * hb, nhb - 1), 0)),
            pl.BlockSpec((1, T, QK_WIDTH), tile),
            pl.BlockSpec((1, T, V_WIDTH), tile),
            pl.BlockSpec((1, cpt, HEADS, DK, DV), lambda b, i: (b, i, 0, 0, 0)),
            pl.BlockSpec((cpt, 3, DK), lambda b, i: (i, 0, 0)),
            _const_spec(rot_off),
            _layer_spec(dl, layer),
            _layer_spec(w_in, layer),
            _layer_spec(pool_w, layer),
            _layer_spec(pool_scale, layer),
            _layer_spec(w_pool_out, layer),
            _layer_spec(w_ret_out, layer),
            _layer_spec(w_o, layer),
        ],
        out_specs=pl.BlockSpec((1, T, D), tile),
        out_shape=jax.ShapeDtypeStruct((B, L, D), F32),
        scratch_shapes=[
            pltpu.VMEM((HEADS, DK, DV), F32),
            pltpu.VMEM((HEADS, CHUNK, CHUNK), F32),
            pltpu.VMEM((HEADS, CHUNK, 2 * DK), F32),
            pltpu.VMEM((HEADS, CHUNK, DK), F32),
            pltpu.VMEM((HEADS, 8, DV), F32),
            pltpu.VMEM((T + 2 * HALO, D), BF16),
            pltpu.VMEM((T + 2 * HALO, POOL_WIDTH), F32),
            pltpu.VMEM((POOL_WIDTH, D), BF16),
        ],
        compiler_params=_params(("arbitrary", "arbitrary")),
        name="mixer",
    )(*(x if paired else (x,)), h, h, h, k, v, sb, rot_base, rot_off, dl, w_in, pool_w, pool_scale,
      w_pool_out, w_ret_out, w_o)


def _ffn(x, p, layer, g_ffn, w_ffn_in, w_ffn_out, g_ple, w_ple_gate, w_ple_proj, g_out, final_norm,
         batch_offset=0):
    _, L, D = x.shape
    T = T_FFN
    nt = L // T
    tile = lambda b, i: (b, i, 0)
    if final_norm:
        B = p.shape[1]
        p_specs = [pl.BlockSpec((None, 1, T, PLE_DIM), lambda b, i: (layer, b, i, 0))]
        p_args = [p]
        out_specs = pl.BlockSpec((1, T, D), tile)
        out_shape = jax.ShapeDtypeStruct((B, L, D), F32)
    else:
        B = x.shape[0]
        p_specs = _pair_specs((None, 1, T, PLE_DIM), nt, lambda i: i, lead=(layer,))
        p_args = list(p)
        out_specs = [pl.BlockSpec((1, T, D), tile)] * 2
        out_shape = [jax.ShapeDtypeStruct((B, L, D), F32), jax.ShapeDtypeStruct((B, L, D), BF16)]
    return pl.pallas_call(
        functools.partial(_ffn_kernel, final_norm=final_norm),
        grid=(B, nt),
        in_specs=[pl.BlockSpec((1, T, D), lambda b, i: (b + batch_offset, i, 0))] + p_specs + [
            _layer_spec(g_ffn, layer),
            _layer_spec(w_ffn_in, layer),
            _layer_spec(w_ffn_out, layer),
            _layer_spec(g_ple, layer),
            _layer_spec(w_ple_gate, layer),
            _layer_spec(w_ple_proj, layer),
            _const_spec(g_out),
        ],
        out_specs=out_specs,
        out_shape=out_shape,
        compiler_params=_params(("arbitrary", "arbitrary")),
        name="ffn",
    )(x, *p_args, g_ffn, w_ffn_in, w_ffn_out, g_ple, w_ple_gate, w_ple_proj, g_out)


def _rotary_tables(L):
    half = DK // 2
    inv = 1.0 / (ROPE_BASE ** (jnp.arange(half, dtype=F32) / half))
    base = (jnp.arange(L // CHUNK) * CHUNK).astype(F32)[:, None] * inv
    off = jnp.arange(CHUNK).astype(F32)[:, None] * inv
    planes = lambda c, s: jnp.stack([jnp.concatenate([c, c], -1), jnp.concatenate([s, s], -1),
                                     jnp.concatenate([-s, s], -1)])
    rot_base = jnp.transpose(planes(jnp.cos(base), jnp.sin(base)), (1, 0, 2))
    return rot_base, planes(jnp.cos(off), jnp.sin(off))


def kernel(x_prompt, x_sample, p_prompt, p_sample, g_mix, w_in, pool_w, pool_scale, decay_logit,
           w_pool_out, w_ret_out, w_o, g_ffn, w_ffn_in, w_ffn_out, g_ple, w_ple_gate, w_ple_proj,
           g_final):
    depth = w_in.shape[0]
    L = x_prompt.shape[1]
    rot_base, rot_off = _rotary_tables(L)
    w_in_b = w_in.astype(BF16)
    w_ret_out_b = w_ret_out.astype(BF16)
    w_o_b = w_o.astype(BF16)
    w_ffn_in_b = w_ffn_in.astype(BF16)
    w_ffn_out_b = w_ffn_out.astype(BF16)
    w_ple_gate_b = w_ple_gate.astype(BF16)
    w_ple_proj_b = w_ple_proj.astype(BF16)
    dl = jnp.broadcast_to(decay_logit.astype(F32)[..., None, None], decay_logit.shape + (1, DV))
    rows = lambda v: v.reshape(v.shape[0], 1, v.shape[1])
    g_mix_r, g_ffn_r, g_ple_r, pool_scale_r = rows(g_mix), rows(g_ffn), rows(g_ple), rows(pool_scale)
    g_final_r = g_final.reshape(1, -1)

    x = (x_prompt, x_sample)
    p = (p_prompt, p_sample)
    ffn_w = (g_ffn_r, w_ffn_in_b, w_ffn_out_b, g_ple_r, w_ple_gate_b, w_ple_proj_b)
    for l in range(depth):
        if l == 0:
            h, k, v, sb = _kv(x, l, g_mix_r, w_in_b, rot_base, rot_off, dl, True)
        else:
            k, v, sb = _kv(h, l, g_mix_r, w_in_b, rot_base, rot_off, dl, False)
        x = _mixer(x, h, k, v, sb, l, w_in_b, pool_w, pool_scale_r, w_pool_out,
                   w_ret_out_b, w_o_b, rot_base, rot_off, dl)
        if l < depth - 1:
            x, h = _ffn(x, p, l, *ffn_w, g_mix_r[l + 1], False)
    nb = x_prompt.shape[0]
    return (_ffn(x, p_prompt, depth - 1, *ffn_w, g_final_r, True),
            _ffn(x, p_sample, depth - 1, *ffn_w, g_final_r, True, batch_offset=nb))
```

```python
import functools

import jax
import jax.numpy as jnp
from jax import lax
from jax.experimental import pallas as pl
from jax.experimental.pallas import tpu as pltpu

F32 = jnp.float32
BF16 = jnp.bfloat16

D_MODEL = 1024
PLE_DIM = 256
POOL_GROUP_DIM = 128
POOL_WIDTH = 512
POOL_WINDOWS = (2, 4, 8, 16)
HEADS = 4
DK = 128
DV = 256
QK_WIDTH = HEADS * DK
V_WIDTH = HEADS * DV
D_FF = 2816
ROPE_BASE = 10000.0
RMS_EPS = 1e-6
GN_EPS = 1e-5

OFF_Q = POOL_WIDTH
OFF_K = OFF_Q + QK_WIDTH
OFF_V = OFF_K + QK_WIDTH
OFF_G = OFF_V + V_WIDTH
OFF_GATE = OFF_G + V_WIDTH

HALO = 16
CHUNK = 256
T_KV = 1024
T_MIX = 512
T_FFN = 1024
FFN_GROUPS = 4
VMEM_LIMIT = 60 * 1024 * 1024
FWD, BWD = 0, 1


def _rms(x, g):
    return x * lax.rsqrt(jnp.mean(x * x, axis=-1, keepdims=True) + RMS_EPS) * g


def _rot(x, c, s):
    return x * c + pltpu.roll(x, DK // 2, 1) * s


def _rotary_chunk(bt_ref, ot_ref, ci):
    cb, sb, sbs = bt_ref[ci, 0:1, :], bt_ref[ci, 1:2, :], bt_ref[ci, 2:3, :]
    co, so, sos = ot_ref[0], ot_ref[1], ot_ref[2]
    return cb * co - sb * so, sbs * co + cb * sos


def _dot(a, b):
    return jnp.dot(a, b, preferred_element_type=F32)


def _dot_tn(a, b):
    return lax.dot_general(a, b, (((0,), (0,)), ((), ())), preferred_element_type=F32)


def _dot_nt(a, b):
    return lax.dot_general(a, b, (((1,), (1,)), ((), ())), preferred_element_type=F32)


def _log_decay(dl_ref, direction, h):
    x = dl_ref[direction, h]
    return jnp.minimum(x, 0.0) - jnp.log1p(jnp.exp(-jnp.abs(x)))


def _kv_kernel(*refs, normalize):
    if normalize:
        (xp_ref, xs_ref, g_ref, wk_ref, wv0_ref, wv1_ref, bt_ref, ot_ref, dl_ref,
         h_ref, k_ref, v_ref, sb_ref, S_ref, kdb_ref, cdb_ref) = refs
    else:
        (x_ref, wk_ref, wv0_ref, wv1_ref, bt_ref, ot_ref, dl_ref,
         k_ref, v_ref, sb_ref, S_ref, kdb_ref, cdb_ref) = refs
    b = pl.program_id(0)
    i = pl.program_id(1)

    @pl.when((b == 0) & (i == 0))
    def _():
        r = lax.broadcasted_iota(jnp.int32, (CHUNK, DK), 0).astype(F32)
        for hd in range(HEADS):
            lgb = _log_decay(dl_ref, BWD, hd)
            kdb_ref[hd] = jnp.exp(r * lgb[:, :DK]) * DK ** -0.5
            cdb_ref[hd] = jnp.broadcast_to(jnp.exp(float(CHUNK) * lgb), (8, DV))

    @pl.when(i == 0)
    def _():
        S_ref[...] = jnp.zeros_like(S_ref)

    chunks = [slice(ci * CHUNK, (ci + 1) * CHUNK) for ci in reversed(range(k_ref.shape[1] // CHUNK))]
    ks, vs = [], []
    for rows in chunks:
        if normalize:
            x = jnp.where(b == 0, xp_ref[0, rows, :], xs_ref[0, rows, :])
            h = _rms(x, g_ref[...]).astype(BF16)
            h_ref[0, rows, :] = h
        else:
            h = x_ref[0, rows, :]
        ks.append(_dot(h, wk_ref[...]))
        v = jnp.concatenate([_dot(h, wv0_ref[...]), _dot(h, wv1_ref[...])], axis=-1).astype(BF16)
        v_ref[0, rows, :] = v
        vs.append(v)
    for rows, k, v in zip(chunks, ks, vs):
        ci = rows.start // CHUNK
        c, s = _rotary_chunk(bt_ref, ot_ref, ci)
        for hd in range(HEADS):
            kh = _rot(k[:, hd * DK:(hd + 1) * DK], c, s)
            k_ref[0, rows, hd * DK:(hd + 1) * DK] = kh.astype(BF16)
            S = S_ref[hd]
            sb_ref[0, ci, hd] = S.astype(BF16)
            upd = _dot_tn((kh * kdb_ref[hd]).astype(BF16), v[:, hd * DV:(hd + 1) * DV])
            S_ref[hd] = S * cdb_ref[hd][0:1, :] + upd


def _mixer_kernel(*refs, seq_len, paired):
    xs_ref = None
    if paired:
        x_ref, xs_ref, *refs = refs
    else:
        x_ref, *refs = refs
    (h_ref, hp_ref, hn_ref, k_ref, v_ref, sb_ref, bt_ref, ot_ref, dl_ref,
     win_ref, pw_ref, ps_ref, wpo_ref, wro_ref, wo_ref, o_ref,
     S_ref, mask_ref, qd_ref, kdf_ref, cdf_ref, hext_ref, xa_ref, wpool_ref, scm_ref, qdec_ref) = refs
    b = pl.program_id(0)
    i = pl.program_id(1)
    nt = pl.num_programs(1)
    T = h_ref.shape[1]
    C = CHUNK
    chunks = [slice(ci * C, (ci + 1) * C) for ci in range(T // C)]
    heads = [slice(hd * DK, (hd + 1) * DK) for hd in range(HEADS)]
    vheads = [slice(hd * DV, (hd + 1) * DV) for hd in range(HEADS)]

    @pl.when((b == 0) & (i == 0))
    def _():
        ri = lax.broadcasted_iota(jnp.int32, (C, C), 0).astype(F32)
        ci = lax.broadcasted_iota(jnp.int32, (C, C), 1).astype(F32)
        r = lax.broadcasted_iota(jnp.int32, (C, DK), 0).astype(F32)
        for hd in range(HEADS):
            lgf = _log_decay(dl_ref, FWD, hd)
            lgb = _log_decay(dl_ref, BWD, hd)
            dfw = ri - ci
            m = jnp.where(dfw >= 0.0,
                          jnp.exp(lgf[:, :C] * jnp.maximum(dfw, 0.0)),
                          jnp.exp(lgb[:, :C] * jnp.maximum(-dfw, 0.0)))
            mask_ref[hd] = m * DK ** -0.5
            qd_ref[hd, :, 0:DK] = jnp.exp((r + 1.0) * lgf[:, :DK])
            qd_ref[hd, :, DK:2 * DK] = jnp.exp((C - r) * lgb[:, :DK])
            kdf_ref[hd] = jnp.exp((C - 1.0 - r) * lgf[:, :DK]) * DK ** -0.5
            cdf_ref[hd] = jnp.broadcast_to(jnp.exp(float(C) * lgf), (8, DV))
        for gi in range(len(POOL_WINDOWS)):
            cols = slice(gi * POOL_GROUP_DIM, (gi + 1) * POOL_GROUP_DIM)
            wpool_ref[cols, :] = jnp.dot(pw_ref[gi] * ps_ref[:, cols], wpo_ref[cols, :],
                                         precision=lax.Precision.HIGHEST,
                                         preferred_element_type=F32).astype(BF16)

    @pl.when(i == 0)
    def _():
        S_ref[...] = jnp.zeros_like(S_ref)

    h = h_ref[0]
    q = _dot(h, win_ref[:, OFF_Q:OFF_K])

    hext_ref[0:HALO, :] = jnp.where(i > 0, hp_ref[0], jnp.zeros_like(hp_ref[0]))
    hext_ref[HALO:HALO + T, :] = h
    hext_ref[HALO + T:, :] = jnp.where(i < nt - 1, hn_ref[0], jnp.zeros_like(hn_ref[0]))
    xa_ref[...] = _dot(hext_ref[...], win_ref[:, 0:POOL_WIDTH])

    rot = [_rotary_chunk(bt_ref, ot_ref, ci) for ci in range(len(chunks))]
    dyn0 = jnp.minimum(i, 0)
    for ci, rows in enumerate(chunks):
        for hd in range(HEADS):
            qh = _rot(q[rows, heads[hd]], *rot[ci])
            qdec_ref[ci + dyn0, hd] = (jnp.concatenate([qh, qh], axis=-1) * qd_ref[hd]).astype(BF16)
            sc = _dot_nt(qh.astype(BF16), k_ref[0, rows, heads[hd]])
            scm_ref[ci + dyn0, hd] = (sc * mask_ref[hd]).astype(BF16)

    xg = _dot(h_ref[0], win_ref[:, OFF_G:OFF_GATE])

    ys = []
    for ci, rows in enumerate(chunks):
        ys.append([])
        for hd in range(HEADS):
            vb = v_ref[0, rows, vheads[hd]]
            S = S_ref[hd]
            states = jnp.concatenate([S.astype(BF16), sb_ref[0, ci, hd]], axis=0)
            ys[ci].append(_dot(scm_ref[ci + dyn0, hd], vb) + _dot(qdec_ref[ci + dyn0, hd], states))
            kf = (k_ref[0, rows, heads[hd]].astype(F32) * kdf_ref[hd]).astype(BF16)
            S_ref[hd] = S * cdf_ref[hd][0:1, :] + _dot_tn(kf, vb)

    gz = _dot(h_ref[0], win_ref[:, OFF_GATE:])

    tglob = i * T + lax.broadcasted_iota(jnp.int32, (T, POOL_GROUP_DIM), 0)
    diffs = []
    for gi, w in enumerate(POOL_WINDOWS):
        cols = slice(gi * POOL_GROUP_DIM, (gi + 1) * POOL_GROUP_DIM)
        acc = xa_ref[HALO - w // 2:HALO - w // 2 + T, cols]
        for kk in range(-w // 2 + 1, w // 2):
            acc = acc + xa_ref[HALO + kk:HALO + kk + T, cols]
        lo = jnp.maximum(tglob - w // 2, 0)
        hi = jnp.minimum(tglob + w // 2, seq_len)
        mean = acc / (hi - lo).astype(F32)
        diffs.append((mean - xa_ref[HALO:HALO + T, cols]).astype(BF16))
    a = _dot(jnp.concatenate(diffs, axis=-1), wpool_ref[...])

    rs = []
    for ci, rows in enumerate(chunks):
        rin = []
        for hd in range(HEADS):
            y = ys[ci][hd]
            yc = y - jnp.mean(y, axis=-1, keepdims=True)
            yn = yc * lax.rsqrt(jnp.mean(yc * yc, axis=-1, keepdims=True) + GN_EPS)
            gh = xg[rows, vheads[hd]]
            rin.append((yn * (gh * jax.nn.sigmoid(gh))).astype(BF16))
        rs.append(_dot(jnp.concatenate(rin, axis=-1), wro_ref[...]))

    for rows, r in zip(chunks, rs):
        gates = jax.nn.sigmoid(gz[rows])
        merged = gates[:, :D_MODEL] * a[rows] + gates[:, D_MODEL:] * r
        x = x_ref[0, rows, :]
        if paired:
            x = jnp.where(b == 0, x, xs_ref[0, rows, :])
        o_ref[0, rows, :] = x + _dot(merged.astype(BF16), wo_ref[...])


def _ffn_kernel(*refs, final_norm):
    if final_norm:
        x_ref, p_ref, gf_ref, wfi_ref, wfo_ref, gp_ref, wpg_ref, wpp_ref, gout_ref, o_ref = refs
    else:
        (x_ref, p_ref, ps_ref, gf_ref, wfi_ref, wfo_ref, gp_ref, wpg_ref, wpp_ref, gout_ref,
         o_ref, hn_ref) = refs
    b = pl.program_id(0)
    rpg = x_ref.shape[1] // FFN_GROUPS
    groups = [slice(gi * rpg, (gi + 1) * rpg) for gi in range(FFN_GROUPS)]
    xs = [x_ref[0, rows, :] for rows in groups]
    gt, up, pp = [], [], []
    for x, rows in zip(xs, groups):
        h2 = _rms(x, gf_ref[...]).astype(BF16)
        gt.append(_dot(h2, wfi_ref[:, :D_FF]))
        up.append(_dot(h2, wfi_ref[:, D_FF:]))
        p = p_ref[0, rows, :]
        if not final_norm:
            p = jnp.where(b == 0, p, ps_ref[0, rows, :])
        pp.append(_dot(p.astype(BF16), wpp_ref[...]))
    xs = [x + _dot((g * jax.nn.sigmoid(g) * u).astype(BF16), wfo_ref[...])
          for x, g, u in zip(xs, gt, up)]
    pg = [jax.nn.sigmoid(_dot(_rms(x, gp_ref[...]).astype(BF16), wpg_ref[...])) for x in xs]
    for x, rows, proj, gate in zip(xs, groups, pp, pg):
        x = x + proj * gate
        if final_norm:
            o_ref[0, rows, :] = _rms(x, gout_ref[...])
        else:
            o_ref[0, rows, :] = x
            hn_ref[0, rows, :] = _rms(x, gout_ref[...]).astype(BF16)


def _layer_spec(arr, layer):
    nd = arr.ndim - 1
    return pl.BlockSpec((None,) + arr.shape[1:], lambda b, i: (layer,) + (0,) * nd,
                        pipeline_mode=pl.Buffered(1))


def _const_spec(arr):
    nd = arr.ndim
    return pl.BlockSpec(arr.shape, lambda b, i: (0,) * nd, pipeline_mode=pl.Buffered(1))


def _pair_specs(block, nt, tile_of, lead=()):
    first, last = tile_of(0), tile_of(nt - 1)
    prompt = pl.BlockSpec(block, lambda b, i: lead + (0, jnp.where(b == 0, tile_of(i), last), 0))
    sample = pl.BlockSpec(block, lambda b, i: lead + (jnp.maximum(b - 1, 0),
                                                      jnp.where(b == 0, first, tile_of(i)), 0))
    return [prompt, sample]


def _params(semantics):
    return pltpu.CompilerParams(dimension_semantics=semantics, vmem_limit_bytes=VMEM_LIMIT)


def _kv(x, layer, g_mix, w_in, rot_base, rot_off, dl, normalize):
    B = x[0].shape[0] + x[1].shape[0] if normalize else x.shape[0]
    _, L, D = x[0].shape if normalize else x.shape
    T = T_KV
    nt = L // T
    cpt = T // CHUNK
    rev = lambda b, i: (b, nt - 1 - i, 0)
    wcols = lambda off: pl.BlockSpec((None, D, QK_WIDTH), lambda b, i: (layer, 0, off // QK_WIDTH),
                                     pipeline_mode=pl.Buffered(1))
    out_specs, out_shape = [], []
    if normalize:
        in_specs = _pair_specs((1, T, D), nt, lambda i: nt - 1 - i) + [_layer_spec(g_mix, layer)]
        args = [x[0], x[1], g_mix]
        out_specs.append(pl.BlockSpec((1, T, D), rev))
        out_shape.append(jax.ShapeDtypeStruct((B, L, D), BF16))
    else:
        in_specs = [pl.BlockSpec((1, T, D), rev)]
        args = [x]
    in_specs += [
        wcols(OFF_K),
        wcols(OFF_V),
        wcols(OFF_V + QK_WIDTH),
        pl.BlockSpec((cpt, 3, DK), lambda b, i: (nt - 1 - i, 0, 0)),
        _const_spec(rot_off),
        _layer_spec(dl, layer),
    ]
    args += [w_in, w_in, w_in, rot_base, rot_off, dl]
    out_specs += [
        pl.BlockSpec((1, T, QK_WIDTH), rev),
        pl.BlockSpec((1, T, V_WIDTH), rev),
        pl.BlockSpec((1, cpt, HEADS, DK, DV), lambda b, i: (b, nt - 1 - i, 0, 0, 0)),
    ]
    out_shape += [
        jax.ShapeDtypeStruct((B, L, QK_WIDTH), BF16),
        jax.ShapeDtypeStruct((B, L, V_WIDTH), BF16),
        jax.ShapeDtypeStruct((B, L // CHUNK, HEADS, DK, DV), BF16),
    ]
    return pl.pallas_call(
        functools.partial(_kv_kernel, normalize=normalize),
        grid=(B, nt),
        in_specs=in_specs,
        out_specs=out_specs,
        out_shape=out_shape,
        scratch_shapes=[
            pltpu.VMEM((HEADS, DK, DV), F32),
            pltpu.VMEM((HEADS, CHUNK, DK), F32),
            pltpu.VMEM((HEADS, 8, DV), F32),
        ],
        compiler_params=_params(("arbitrary", "arbitrary")),
        name="kv",
    )(*args)


def _mixer(x, h, k, v, sb, layer, w_in, pool_w, pool_scale, w_pool_out, w_ret_out, w_o,
           rot_base, rot_off, dl):
    paired = isinstance(x, tuple)
    B, L, D = h.shape
    T = T_MIX
    nt = L // T
    cpt = T // CHUNK
    hb = T // HALO
    nhb = L // HALO
    tile = lambda b, i: (b, i, 0)
    return pl.pallas_call(
        functools.partial(_mixer_kernel, seq_len=L, paired=paired),
        grid=(B, nt),
        in_specs=(_pair_specs((1, T, D), nt, lambda i: i) if paired
                  else [pl.BlockSpec((1, T, D), tile)]) + [
            pl.BlockSpec((1, T, D), tile),
            pl.BlockSpec((1, HALO, D), lambda b, i: (b, jnp.maximum(i * hb - 1, 0), 0)),
            pl.BlockSpec((1, HALO, D), lambda b, i: (b, jnp.minimum((i + 1) * hb, nhb - 1), 0)),
            pl.BlockSpec((1, T, QK_WIDTH), tile),
            pl.BlockSpec((1, T, V_WIDTH), tile),
            pl.BlockSpec((1, cpt, HEADS, DK, DV), lambda b, i: (b, i, 0, 0, 0)),
            pl.BlockSpec((cpt, 3, DK), lambda b, i: (i, 0, 0)),
            _const_spec(rot_off),
            _layer_spec(dl, layer),
            _layer_spec(w_in, layer),
            _layer_spec(pool_w, layer),
            _layer_spec(pool_scale, layer),
            _layer_spec(w_pool_out, layer),
            _layer_spec(w_ret_out, layer),
            _layer_spec(w_o, layer),
        ],
        out_specs=pl.BlockSpec((1, T, D), tile),
        out_shape=jax.ShapeDtypeStruct((B, L, D), F32),
        scratch_shapes=[
            pltpu.VMEM((HEADS, DK, DV), F32),
            pltpu.VMEM((HEADS, CHUNK, CHUNK), F32),
            pltpu.VMEM((HEADS, CHUNK, 2 * DK), F32),
            pltpu.VMEM((HEADS, CHUNK, DK), F32),
            pltpu.VMEM((HEADS, 8, DV), F32),
            pltpu.VMEM((T + 2 * HALO, D), BF16),
            pltpu.VMEM((T + 2 * HALO, POOL_WIDTH), F32),
            pltpu.VMEM((POOL_WIDTH, D), BF16),
            pltpu.VMEM((cpt, HEADS, CHUNK, CHUNK), BF16),
            pltpu.VMEM((cpt, HEADS, CHUNK, 2 * DK), BF16),
        ],
        compiler_params=_params(("arbitrary", "arbitrary")),
        name="mixer",
    )(*(x if paired else (x,)), h, h, h, k, v, sb, rot_base, rot_off, dl, w_in, pool_w, pool_scale,
      w_pool_out, w_ret_out, w_o)


def _ffn(x, p, layer, g_ffn, w_ffn_in, w_ffn_out, g_ple, w_ple_gate, w_ple_proj, g_out, final_norm,
         batch_offset=0):
    _, L, D = x.shape
    T = T_FFN
    nt = L // T
    tile = lambda b, i: (b, i, 0)
    if final_norm:
        B = p.shape[1]
        p_specs = [pl.BlockSpec((None, 1, T, PLE_DIM), lambda b, i: (layer, b, i, 0))]
        p_args = [p]
        out_specs = pl.BlockSpec((1, T, D), tile)
        out_shape = jax.ShapeDtypeStruct((B, L, D), F32)
    else:
        B = x.shape[0]
        p_specs = _pair_specs((None, 1, T, PLE_DIM), nt, lambda i: i, lead=(layer,))
        p_args = list(p)
        out_specs = [pl.BlockSpec((1, T, D), tile)] * 2
        out_shape = [jax.ShapeDtypeStruct((B, L, D), F32), jax.ShapeDtypeStruct((B, L, D), BF16)]
    return pl.pallas_call(
        functools.partial(_ffn_kernel, final_norm=final_norm),
        grid=(B, nt),
        in_specs=[pl.BlockSpec((1, T, D), lambda b, i: (b + batch_offset, i, 0))] + p_specs + [
            _layer_spec(g_ffn, layer),
            _layer_spec(w_ffn_in, layer),
            _layer_spec(w_ffn_out, layer),
            _layer_spec(g_ple, layer),
            _layer_spec(w_ple_gate, layer),
            _layer_spec(w_ple_proj, layer),
            _const_spec(g_out),
        ],
        out_specs=out_specs,
        out_shape=out_shape,
        compiler_params=_params(("arbitrary", "arbitrary")),
        name="ffn",
    )(x, *p_args, g_ffn, w_ffn_in, w_ffn_out, g_ple, w_ple_gate, w_ple_proj, g_out)


def _rotary_tables(L):
    half = DK // 2
    inv = 1.0 / (ROPE_BASE ** (jnp.arange(half, dtype=F32) / half))
    base = (jnp.arange(L // CHUNK) * CHUNK).astype(F32)[:, None] * inv
    off = jnp.arange(CHUNK).astype(F32)[:, None] * inv
    planes = lambda c, s: jnp.stack([jnp.concatenate([c, c], -1), jnp.concatenate([s, s], -1),
                                     jnp.concatenate([-s, s], -1)])
    rot_base = jnp.transpose(planes(jnp.cos(base), jnp.sin(base)), (1, 0, 2))
    return rot_base, planes(jnp.cos(off), jnp.sin(off))


def kernel(x_prompt, x_sample, p_prompt, p_sample, g_mix, w_in, pool_w, pool_scale, decay_logit,
           w_pool_out, w_ret_out, w_o, g_ffn, w_ffn_in, w_ffn_out, g_ple, w_ple_gate, w_ple_proj,
           g_final):
    depth = w_in.shape[0]
    L = x_prompt.shape[1]
    rot_base, rot_off = _rotary_tables(L)
    w_in_b = w_in.astype(BF16)
    w_ret_out_b = w_ret_out.astype(BF16)
    w_o_b = w_o.astype(BF16)
    w_ffn_in_b = w_ffn_in.astype(BF16)
    w_ffn_out_b = w_ffn_out.astype(BF16)
    w_ple_gate_b = w_ple_gate.astype(BF16)
    w_ple_proj_b = w_ple_proj.astype(BF16)
    dl = jnp.broadcast_to(decay_logit.astype(F32)[..., None, None], decay_logit.shape + (1, DV))
    rows = lambda v: v.reshape(v.shape[0], 1, v.shape[1])
    g_mix_r, g_ffn_r, g_ple_r, pool_scale_r = rows(g_mix), rows(g_ffn), rows(g_ple), rows(pool_scale)
    g_final_r = g_final.reshape(1, -1)

    x = (x_prompt, x_sample)
    p = (p_prompt, p_sample)
    ffn_w = (g_ffn_r, w_ffn_in_b, w_ffn_out_b, g_ple_r, w_ple_gate_b, w_ple_proj_b)
    for l in range(depth):
        if l == 0:
            h, k, v, sb = _kv(x, l, g_mix_r, w_in_b, rot_base, rot_off, dl, True)
        else:
            k, v, sb = _kv(h, l, g_mix_r, w_in_b, rot_base, rot_off, dl, False)
        x = _mixer(x, h, k, v, sb, l, w_in_b, pool_w, pool_scale_r, w_pool_out,
                   w_ret_out_b, w_o_b, rot_base, rot_off, dl)
        if l < depth - 1:
            x, h = _ffn(x, p, l, *ffn_w, g_mix_r[l + 1], False)
    nb = x_prompt.shape[0]
    return (_ffn(x, p_prompt, depth - 1, *ffn_w, g_final_r, True),
            _ffn(x, p_sample, depth - 1, *ffn_w, g_final_r, True, batch_offset=nb))
```

```python
import functools

import jax
import jax.numpy as jnp
from jax import lax
from jax.experimental import pallas as pl
from jax.experimental.pallas import tpu as pltpu

F32 = jnp.float32
BF16 = jnp.bfloat16

D_MODEL = 1024
PLE_DIM = 256
POOL_GROUP_DIM = 128
POOL_WIDTH = 512
POOL_WINDOWS = (2, 4, 8, 16)
HEADS = 4
DK = 128
DV = 256
QK_WIDTH = HEADS * DK
V_WIDTH = HEADS * DV
D_FF = 2816
ROPE_BASE = 10000.0
RMS_EPS = 1e-6
GN_EPS = 1e-5

OFF_Q = POOL_WIDTH
OFF_K = OFF_Q + QK_WIDTH
OFF_V = OFF_K + QK_WIDTH
OFF_G = OFF_V + V_WIDTH
OFF_GATE = OFF_G + V_WIDTH

HALO = 16
CHUNK = 256
T_KV = 1024
T_MIX = 512
T_FFN = 1024
FFN_GROUPS = 4
VMEM_LIMIT = 60 * 1024 * 1024
FWD, BWD = 0, 1


def _rms(x, g):
    return x * lax.rsqrt(jnp.mean(x * x, axis=-1, keepdims=True) + RMS_EPS) * g


def _sigmoid(x):
    return 0.5 * jnp.tanh(0.5 * x) + 0.5


def _silu(x):
    t = 0.5 * x
    return t * jnp.tanh(t) + t


def _rot(x, c, s):
    return x * c + pltpu.roll(x, DK // 2, 1) * s


def _rotary_chunk(bt_ref, ot_ref, ci):
    cb, sb, sbs = bt_ref[ci, 0:1, :], bt_ref[ci, 1:2, :], bt_ref[ci, 2:3, :]
    co, so, sos = ot_ref[0], ot_ref[1], ot_ref[2]
    return cb * co - sb * so, sbs * co + cb * sos


def _dot(a, b):
    return jnp.dot(a, b, preferred_element_type=F32)


def _dot_tn(a, b):
    return lax.dot_general(a, b, (((0,), (0,)), ((), ())), preferred_element_type=F32)


def _dot_nt(a, b):
    return lax.dot_general(a, b, (((1,), (1,)), ((), ())), preferred_element_type=F32)


def _log_decay(dl_ref, direction, h):
    x = dl_ref[direction, h]
    return jnp.minimum(x, 0.0) - jnp.log1p(jnp.exp(-jnp.abs(x)))


def _kv_kernel(*refs, normalize):
    if normalize:
        (xp_ref, xs_ref, g_ref, wk_ref, wv0_ref, wv1_ref, bt_ref, ot_ref, dl_ref,
         h_ref, k_ref, v_ref, sb_ref, S_ref, kdb_ref, cdb_ref) = refs
    else:
        (x_ref, wk_ref, wv0_ref, wv1_ref, bt_ref, ot_ref, dl_ref,
         k_ref, v_ref, sb_ref, S_ref, kdb_ref, cdb_ref) = refs
    b = pl.program_id(0)
    i = pl.program_id(1)

    @pl.when((b == 0) & (i == 0))
    def _():
        r = lax.broadcasted_iota(jnp.int32, (CHUNK, DK), 0).astype(F32)
        for hd in range(HEADS):
            lgb = _log_decay(dl_ref, BWD, hd)
            kdb_ref[hd] = jnp.exp(r * lgb[:, :DK]) * DK ** -0.5
            cdb_ref[hd] = jnp.broadcast_to(jnp.exp(float(CHUNK) * lgb), (8, DV))

    @pl.when(i == 0)
    def _():
        S_ref[...] = jnp.zeros_like(S_ref)

    chunks = [slice(ci * CHUNK, (ci + 1) * CHUNK) for ci in reversed(range(k_ref.shape[1] // CHUNK))]
    ks, vs = [], []
    for rows in chunks:
        if normalize:
            x = jnp.where(b == 0, xp_ref[0, rows, :], xs_ref[0, rows, :])
            h = _rms(x, g_ref[...]).astype(BF16)
            h_ref[0, rows, :] = h
        else:
            h = x_ref[0, rows, :]
        ks.append(_dot(h, wk_ref[...]))
        v = jnp.concatenate([_dot(h, wv0_ref[...]), _dot(h, wv1_ref[...])], axis=-1).astype(BF16)
        v_ref[0, rows, :] = v
        vs.append(v)
    for rows, k, v in zip(chunks, ks, vs):
        ci = rows.start // CHUNK
        c, s = _rotary_chunk(bt_ref, ot_ref, ci)
        for hd in range(HEADS):
            kh = _rot(k[:, hd * DK:(hd + 1) * DK], c, s)
            k_ref[0, rows, hd * DK:(hd + 1) * DK] = kh.astype(BF16)
            S = S_ref[hd]
            sb_ref[0, ci, hd] = S.astype(BF16)
            upd = _dot_tn((kh * kdb_ref[hd]).astype(BF16), v[:, hd * DV:(hd + 1) * DV])
            S_ref[hd] = S * cdb_ref[hd][0:1, :] + upd


def _mixer_kernel(*refs, seq_len, paired):
    xs_ref = None
    if paired:
        x_ref, xs_ref, *refs = refs
    else:
        x_ref, *refs = refs
    (h_ref, hp_ref, hn_ref, k_ref, v_ref, sb_ref, bt_ref, ot_ref, dl_ref,
     win_ref, pw_ref, ps_ref, wpo_ref, wro_ref, wo_ref, o_ref,
     S_ref, mask_ref, qd_ref, kdf_ref, cdf_ref, hext_ref, xa_ref, wpool_ref, scm_ref, qdec_ref) = refs
    b = pl.program_id(0)
    i = pl.program_id(1)
    nt = pl.num_programs(1)
    T = h_ref.shape[1]
    C = CHUNK
    chunks = [slice(ci * C, (ci + 1) * C) for ci in range(T // C)]
    heads = [slice(hd * DK, (hd + 1) * DK) for hd in range(HEADS)]
    vheads = [slice(hd * DV, (hd + 1) * DV) for hd in range(HEADS)]

    @pl.when((b == 0) & (i == 0))
    def _():
        ri = lax.broadcasted_iota(jnp.int32, (C, C), 0).astype(F32)
        ci = lax.broadcasted_iota(jnp.int32, (C, C), 1).astype(F32)
        r = lax.broadcasted_iota(jnp.int32, (C, DK), 0).astype(F32)
        for hd in range(HEADS):
            lgf = _log_decay(dl_ref, FWD, hd)
            lgb = _log_decay(dl_ref, BWD, hd)
            dfw = ri - ci
            m = jnp.where(dfw >= 0.0,
                          jnp.exp(lgf[:, :C] * jnp.maximum(dfw, 0.0)),
                          jnp.exp(lgb[:, :C] * jnp.maximum(-dfw, 0.0)))
            mask_ref[hd] = m * DK ** -0.5
            qd_ref[hd, :, 0:DK] = jnp.exp((r + 1.0) * lgf[:, :DK])
            qd_ref[hd, :, DK:2 * DK] = jnp.exp((C - r) * lgb[:, :DK])
            kdf_ref[hd] = jnp.exp((C - 1.0 - r) * lgf[:, :DK]) * DK ** -0.5
            cdf_ref[hd] = jnp.broadcast_to(jnp.exp(float(C) * lgf), (8, DV))
        for gi in range(len(POOL_WINDOWS)):
            cols = slice(gi * POOL_GROUP_DIM, (gi + 1) * POOL_GROUP_DIM)
            wpool_ref[cols, :] = jnp.dot(pw_ref[gi] * ps_ref[:, cols], wpo_ref[cols, :],
                                         precision=lax.Precision.HIGHEST,
                                         preferred_element_type=F32).astype(BF16)

    @pl.when(i == 0)
    def _():
        S_ref[...] = jnp.zeros_like(S_ref)

    h = h_ref[0]
    q = _dot(h, win_ref[:, OFF_Q:OFF_K])

    hext_ref[0:HALO, :] = jnp.where(i > 0, hp_ref[0], jnp.zeros_like(hp_ref[0]))
    hext_ref[HALO:HALO + T, :] = h
    hext_ref[HALO + T:, :] = jnp.where(i < nt - 1, hn_ref[0], jnp.zeros_like(hn_ref[0]))
    xa_ref[...] = _dot(hext_ref[...], win_ref[:, 0:POOL_WIDTH])

    rot = [_rotary_chunk(bt_ref, ot_ref, ci) for ci in range(len(chunks))]
    dyn0 = jnp.minimum(i, 0)
    for ci, rows in enumerate(chunks):
        for hd in range(HEADS):
            qh = _rot(q[rows, heads[hd]], *rot[ci])
            qdec_ref[ci + dyn0, hd] = (jnp.concatenate([qh, qh], axis=-1) * qd_ref[hd]).astype(BF16)
            sc = _dot_nt(qh.astype(BF16), k_ref[0, rows, heads[hd]])
            scm_ref[ci + dyn0, hd] = (sc * mask_ref[hd]).astype(BF16)

    xg = _dot(h_ref[0], win_ref[:, OFF_G:OFF_GATE])

    ys = []
    for ci, rows in enumerate(chunks):
        ys.append([])
        for hd in range(HEADS):
            vb = v_ref[0, rows, vheads[hd]]
            S = S_ref[hd]
            states = jnp.concatenate([S.astype(BF16), sb_ref[0, ci, hd]], axis=0)
            ys[ci].append(_dot(scm_ref[ci + dyn0, hd], vb) + _dot(qdec_ref[ci + dyn0, hd], states))
            kf = (k_ref[0, rows, heads[hd]].astype(F32) * kdf_ref[hd]).astype(BF16)
            S_ref[hd] = S * cdf_ref[hd][0:1, :] + _dot_tn(kf, vb)

    gz = _dot(h_ref[0], win_ref[:, OFF_GATE:])

    tglob = i * T + lax.broadcasted_iota(jnp.int32, (T, POOL_GROUP_DIM), 0)
    diffs = []
    for gi, w in enumerate(POOL_WINDOWS):
        cols = slice(gi * POOL_GROUP_DIM, (gi + 1) * POOL_GROUP_DIM)
        acc = xa_ref[HALO - w // 2:HALO - w // 2 + T, cols]
        for kk in range(-w // 2 + 1, w // 2):
            acc = acc + xa_ref[HALO + kk:HALO + kk + T, cols]
        lo = jnp.maximum(tglob - w // 2, 0)
        hi = jnp.minimum(tglob + w // 2, seq_len)
        mean = acc / (hi - lo).astype(F32)
        diffs.append((mean - xa_ref[HALO:HALO + T, cols]).astype(BF16))
    a = _dot(jnp.concatenate(diffs, axis=-1), wpool_ref[...])

    rs = []
    for ci, rows in enumerate(chunks):
        rin = []
        for hd in range(HEADS):
            y = ys[ci][hd]
            yc = y - jnp.mean(y, axis=-1, keepdims=True)
            yn = yc * lax.rsqrt(jnp.mean(yc * yc, axis=-1, keepdims=True) + GN_EPS)
            gh = xg[rows, vheads[hd]]
            rin.append((yn * _silu(gh)).astype(BF16))
        rs.append(_dot(jnp.concatenate(rin, axis=-1), wro_ref[...]))

    for rows, r in zip(chunks, rs):
        gates = _sigmoid(gz[rows])
        merged = gates[:, :D_MODEL] * a[rows] + gates[:, D_MODEL:] * r
        x = x_ref[0, rows, :]
        if paired:
            x = jnp.where(b == 0, x, xs_ref[0, rows, :])
        o_ref[0, rows, :] = x + _dot(merged.astype(BF16), wo_ref[...])


def _ffn_kernel(*refs, final_norm):
    if final_norm:
        x_ref, p_ref, gf_ref, wfi_ref, wfo_ref, gp_ref, wpg_ref, wpp_ref, gout_ref, o_ref = refs
    else:
        (x_ref, p_ref, ps_ref, gf_ref, wfi_ref, wfo_ref, gp_ref, wpg_ref, wpp_ref, gout_ref,
         o_ref, hn_ref) = refs
    b = pl.program_id(0)
    rpg = x_ref.shape[1] // FFN_GROUPS
    groups = [slice(gi * rpg, (gi + 1) * rpg) for gi in range(FFN_GROUPS)]
    xs = [x_ref[0, rows, :] for rows in groups]
    gt, up, pp = [], [], []
    for x, rows in zip(xs, groups):
        h2 = _rms(x, gf_ref[...]).astype(BF16)
        gt.append(_dot(h2, wfi_ref[:, :D_FF]))
        up.append(_dot(h2, wfi_ref[:, D_FF:]))
        p = p_ref[0, rows, :]
        if not final_norm:
            p = jnp.where(b == 0, p, ps_ref[0, rows, :])
        pp.append(_dot(p.astype(BF16), wpp_ref[...]))
    xs = [x + _dot((_silu(g) * u).astype(BF16), wfo_ref[...])
          for x, g, u in zip(xs, gt, up)]
    pg = [_sigmoid(_dot(_rms(x, gp_ref[...]).astype(BF16), wpg_ref[...])) for x in xs]
    for x, rows, proj, gate in zip(xs, groups, pp, pg):
        x = x + proj * gate
        if final_norm:
            o_ref[0, rows, :] = _rms(x, gout_ref[...])
        else:
            o_ref[0, rows, :] = x
            hn_ref[0, rows, :] = _rms(x, gout_ref[...]).astype(BF16)


def _layer_spec(arr, layer):
    nd = arr.ndim - 1
    return pl.BlockSpec((None,) + arr.shape[1:], lambda b, i: (layer,) + (0,) * nd,
                        pipeline_mode=pl.Buffered(1))


def _const_spec(arr):
    nd = arr.ndim
    return pl.BlockSpec(arr.shape, lambda b, i: (0,) * nd, pipeline_mode=pl.Buffered(1))


def _pair_specs(block, nt, tile_of, lead=()):
    first, last = tile_of(0), tile_of(nt - 1)
    prompt = pl.BlockSpec(block, lambda b, i: lead + (0, jnp.where(b == 0, tile_of(i), last), 0))
    sample = pl.BlockSpec(block, lambda b, i: lead + (jnp.maximum(b - 1, 0),
                                                      jnp.where(b == 0, first, tile_of(i)), 0))
    return [prompt, sample]


def _params(semantics):
    return pltpu.CompilerParams(dimension_semantics=semantics, vmem_limit_bytes=VMEM_LIMIT)


def _kv(x, layer, g_mix, w_in, rot_base, rot_off, dl, normalize):
    B = x[0].shape[0] + x[1].shape[0] if normalize else x.shape[0]
    _, L, D = x[0].shape if normalize else x.shape
    T = T_KV
    nt = L // T
    cpt = T // CHUNK
    rev = lambda b, i: (b, nt - 1 - i, 0)
    wcols = lambda off: pl.BlockSpec((None, D, QK_WIDTH), lambda b, i: (layer, 0, off // QK_WIDTH),
                                     pipeline_mode=pl.Buffered(1))
    out_specs, out_shape = [], []
    if normalize:
        in_specs = _pair_specs((1, T, D), nt, lambda i: nt - 1 - i) + [_layer_spec(g_mix, layer)]
        args = [x[0], x[1], g_mix]
        out_specs.append(pl.BlockSpec((1, T, D), rev))
        out_shape.append(jax.ShapeDtypeStruct((B, L, D), BF16))
    else:
        in_specs = [pl.BlockSpec((1, T, D), rev)]
        args = [x]
    in_specs += [
        wcols(OFF_K),
        wcols(OFF_V),
        wcols(OFF_V + QK_WIDTH),
        pl.BlockSpec((cpt, 3, DK), lambda b, i: (nt - 1 - i, 0, 0)),
        _const_spec(rot_off),
        _layer_spec(dl, layer),
    ]
    args += [w_in, w_in, w_in, rot_base, rot_off, dl]
    out_specs += [
        pl.BlockSpec((1, T, QK_WIDTH), rev),
        pl.BlockSpec((1, T, V_WIDTH), rev),
        pl.BlockSpec((1, cpt, HEADS, DK, DV), lambda b, i: (b, nt - 1 - i, 0, 0, 0)),
    ]
    out_shape += [
        jax.ShapeDtypeStruct((B, L, QK_WIDTH), BF16),
        jax.ShapeDtypeStruct((B, L, V_WIDTH), BF16),
        jax.ShapeDtypeStruct((B, L // CHUNK, HEADS, DK, DV), BF16),
    ]
    return pl.pallas_call(
        functools.partial(_kv_kernel, normalize=normalize),
        grid=(B, nt),
        in_specs=in_specs,
        out_specs=out_specs,
        out_shape=out_shape,
        scratch_shapes=[
            pltpu.VMEM((HEADS, DK, DV), F32),
            pltpu.VMEM((HEADS, CHUNK, DK), F32),
            pltpu.VMEM((HEADS, 8, DV), F32),
        ],
        compiler_params=_params(("arbitrary", "arbitrary")),
        name="kv",
    )(*args)


def _mixer(x, h, k, v, sb, layer, w_in, pool_w, pool_scale, w_pool_out, w_ret_out, w_o,
           rot_base, rot_off, dl):
    paired = isinstance(x, tuple)
    B, L, D = h.shape
    T = T_MIX
    nt = L // T
    cpt = T // CHUNK
    hb = T // HALO
    nhb = L // HALO
    tile = lambda b, i: (b, i, 0)
    return pl.pallas_call(
        functools.partial(_mixer_kernel, seq_len=L, paired=paired),
        grid=(B, nt),
        in_specs=(_pair_specs((1, T, D), nt, lambda i: i) if paired
                  else [pl.BlockSpec((1, T, D), tile)]) + [
            pl.BlockSpec((1, T, D), tile),
            pl.BlockSpec((1, HALO, D), lambda b, i: (b, jnp.maximum(i * hb - 1, 0), 0)),
            pl.BlockSpec((1, HALO, D), lambda b, i: (b, jnp.minimum((i + 1) * hb, nhb - 1), 0)),
            pl.BlockSpec((1, T, QK_WIDTH), tile),
            pl.BlockSpec((1, T, V_WIDTH), tile),
            pl.BlockSpec((1, cpt, HEADS, DK, DV), lambda b, i: (b, i, 0, 0, 0)),
            pl.BlockSpec((cpt, 3, DK), lambda b, i: (i, 0, 0)),
            _const_spec(rot_off),
            _layer_spec(dl, layer),
            _layer_spec(w_in, layer),
            _layer_spec(pool_w, layer),
            _layer_spec(pool_scale, layer),
            _layer_spec(w_pool_out, layer),
            _layer_spec(w_ret_out, layer),
            _layer_spec(w_o, layer),
        ],
        out_specs=pl.BlockSpec((1, T, D), tile),
        out_shape=jax.ShapeDtypeStruct((B, L, D), F32),
        scratch_shapes=[
            pltpu.VMEM((HEADS, DK, DV), F32),
            pltpu.VMEM((HEADS, CHUNK, CHUNK), F32),
            pltpu.VMEM((HEADS, CHUNK, 2 * DK), F32),
            pltpu.VMEM((HEADS, CHUNK, DK), F32),
            pltpu.VMEM((HEADS, 8, DV), F32),
            pltpu.VMEM((T + 2 * HALO, D), BF16),
            pltpu.VMEM((T + 2 * HALO, POOL_WIDTH), F32),
            pltpu.VMEM((POOL_WIDTH, D), BF16),
            pltpu.VMEM((cpt, HEADS, CHUNK, CHUNK), BF16),
            pltpu.VMEM((cpt, HEADS, CHUNK, 2 * DK), BF16),
        ],
        compiler_params=_params(("arbitrary", "arbitrary")),
        name="mixer",
    )(*(x if paired else (x,)), h, h, h, k, v, sb, rot_base, rot_off, dl, w_in, pool_w, pool_scale,
      w_pool_out, w_ret_out, w_o)


def _ffn(x, p, layer, g_ffn, w_ffn_in, w_ffn_out, g_ple, w_ple_gate, w_ple_proj, g_out, final_norm,
         batch_offset=0):
    _, L, D = x.shape
    T = T_FFN
    nt = L // T
    tile = lambda b, i: (b, i, 0)
    if final_norm:
        B = p.shape[1]
        p_specs = [pl.BlockSpec((None, 1, T, PLE_DIM), lambda b, i: (layer, b, i, 0))]
        p_args = [p]
        out_specs = pl.BlockSpec((1, T, D), tile)
        out_shape = jax.ShapeDtypeStruct((B, L, D), F32)
    else:
        B = x.shape[0]
        p_specs = _pair_specs((None, 1, T, PLE_DIM), nt, lambda i: i, lead=(layer,))
        p_args = list(p)
        out_specs = [pl.BlockSpec((1, T, D), tile)] * 2
        out_shape = [jax.ShapeDtypeStruct((B, L, D), F32), jax.ShapeDtypeStruct((B, L, D), BF16)]
    return pl.pallas_call(
        functools.partial(_ffn_kernel, final_norm=final_norm),
        grid=(B, nt),
        in_specs=[pl.BlockSpec((1, T, D), lambda b, i: (b + batch_offset, i, 0))] + p_specs + [
            _layer_spec(g_ffn, layer),
            _layer_spec(w_ffn_in, layer),
            _layer_spec(w_ffn_out, layer),
            _layer_spec(g_ple, layer),
            _layer_spec(w_ple_gate, layer),
            _layer_spec(w_ple_proj, layer),
            _const_spec(g_out),
        ],
        out_specs=out_specs,
        out_shape=out_shape,
        compiler_params=_params(("arbitrary", "arbitrary")),
        name="ffn",
    )(x, *p_args, g_ffn, w_ffn_in, w_ffn_out, g_ple, w_ple_gate, w_ple_proj, g_out)


def _rotary_tables(L):
    half = DK // 2
    inv = 1.0 / (ROPE_BASE ** (jnp.arange(half, dtype=F32) / half))
    base = (jnp.arange(L // CHUNK) * CHUNK).astype(F32)[:, None] * inv
    off = jnp.arange(CHUNK).astype(F32)[:, None] * inv
    planes = lambda c, s: jnp.stack([jnp.concatenate([c, c], -1), jnp.concatenate([s, s], -1),
                                     jnp.concatenate([-s, s], -1)])
    rot_base = jnp.transpose(planes(jnp.cos(base), jnp.sin(base)), (1, 0, 2))
    return rot_base, planes(jnp.cos(off), jnp.sin(off))


def kernel(x_prompt, x_sample, p_prompt, p_sample, g_mix, w_in, pool_w, pool_scale, decay_logit,
           w_pool_out, w_ret_out, w_o, g_ffn, w_ffn_in, w_ffn_out, g_ple, w_ple_gate, w_ple_proj,
           g_final):
    depth = w_in.shape[0]
    L = x_prompt.shape[1]
    rot_base, rot_off = _rotary_tables(L)
    w_in_b = w_in.astype(BF16)
    w_ret_out_b = w_ret_out.astype(BF16)
    w_o_b = w_o.astype(BF16)
    w_ffn_in_b = w_ffn_in.astype(BF16)
    w_ffn_out_b = w_ffn_out.astype(BF16)
    w_ple_gate_b = w_ple_gate.astype(BF16)
    w_ple_proj_b = w_ple_proj.astype(BF16)
    dl = jnp.broadcast_to(decay_logit.astype(F32)[..., None, None], decay_logit.shape + (1, DV))
    rows = lambda v: v.reshape(v.shape[0], 1, v.shape[1])
    g_mix_r, g_ffn_r, g_ple_r, pool_scale_r = rows(g_mix), rows(g_ffn), rows(g_ple), rows(pool_scale)
    g_final_r = g_final.reshape(1, -1)

    x = (x_prompt, x_sample)
    p = (p_prompt, p_sample)
    ffn_w = (g_ffn_r, w_ffn_in_b, w_ffn_out_b, g_ple_r, w_ple_gate_b, w_ple_proj_b)
    for l in range(depth):
        if l == 0:
            h, k, v, sb = _kv(x, l, g_mix_r, w_in_b, rot_base, rot_off, dl, True)
        else:
            k, v, sb = _kv(h, l, g_mix_r, w_in_b, rot_base, rot_off, dl, False)
        x = _mixer(x, h, k, v, sb, l, w_in_b, pool_w, pool_scale_r, w_pool_out,
                   w_ret_out_b, w_o_b, rot_base, rot_off, dl)
        if l < depth - 1:
            x, h = _ffn(x, p, l, *ffn_w, g_mix_r[l + 1], False)
    nb = x_prompt.shape[0]
    return (_ffn(x, p_prompt, depth - 1, *ffn_w, g_final_r, True),
            _ffn(x, p_sample, depth - 1, *ffn_w, g_final_r, True, batch_offset=nb))
```

```python
import functools

import jax
import jax.numpy as jnp
from jax import lax
from jax.experimental import pallas as pl
from jax.experimental.pallas import tpu as pltpu

F32 = jnp.float32
BF16 = jnp.bfloat16

D_MODEL = 1024
PLE_DIM = 256
POOL_GROUP_DIM = 128
POOL_WIDTH = 512
POOL_WINDOWS = (2, 4, 8, 16)
HEADS = 4
DK = 128
DV = 256
QK_WIDTH = HEADS * DK
V_WIDTH = HEADS * DV
D_FF = 2816
ROPE_BASE = 10000.0
RMS_EPS = 1e-6
GN_EPS = 1e-5

OFF_Q = POOL_WIDTH
OFF_K = OFF_Q + QK_WIDTH
OFF_V = OFF_K + QK_WIDTH
OFF_G = OFF_V + V_WIDTH
OFF_GATE = OFF_G + V_WIDTH

HALO = 16
CHUNK = 256
T_KV = 1024
T_MIX = 512
T_FFN = 1024
FFN_GROUPS = 4
VMEM_LIMIT = 60 * 1024 * 1024
FWD, BWD = 0, 1


def _rms(x, g):
    return x * lax.rsqrt(jnp.mean(x * x, axis=-1, keepdims=True) + RMS_EPS) * g


HALF = 0.5


def _sigmoid_of_half(t):
    return 0.5 * jnp.tanh(t) + 0.5


def _silu_of_half(t):
    return t * jnp.tanh(t) + t


def _rot(x, c, s):
    return x * c + pltpu.roll(x, DK // 2, 1) * s


def _rotary_chunk(bt_ref, ot_ref, ci):
    cb, sb, sbs = bt_ref[ci, 0:1, :], bt_ref[ci, 1:2, :], bt_ref[ci, 2:3, :]
    co, so, sos = ot_ref[0], ot_ref[1], ot_ref[2]
    return cb * co - sb * so, sbs * co + cb * sos


def _dot(a, b):
    return jnp.dot(a, b, preferred_element_type=F32)


def _dot_tn(a, b):
    return lax.dot_general(a, b, (((0,), (0,)), ((), ())), preferred_element_type=F32)


def _dot_nt(a, b):
    return lax.dot_general(a, b, (((1,), (1,)), ((), ())), preferred_element_type=F32)


def _log_decay(dl_ref, direction, h):
    x = dl_ref[direction, h]
    return jnp.minimum(x, 0.0) - jnp.log1p(jnp.exp(-jnp.abs(x)))


def _kv_kernel(*refs, normalize):
    if normalize:
        (xp_ref, xs_ref, g_ref, wk_ref, wv0_ref, wv1_ref, bt_ref, ot_ref, dl_ref,
         h_ref, k_ref, v_ref, sb_ref, S_ref, kdb_ref, cdb_ref) = refs
    else:
        (x_ref, wk_ref, wv0_ref, wv1_ref, bt_ref, ot_ref, dl_ref,
         k_ref, v_ref, sb_ref, S_ref, kdb_ref, cdb_ref) = refs
    b = pl.program_id(0)
    i = pl.program_id(1)

    @pl.when((b == 0) & (i == 0))
    def _():
        r = lax.broadcasted_iota(jnp.int32, (CHUNK, DK), 0).astype(F32)
        for hd in range(HEADS):
            lgb = _log_decay(dl_ref, BWD, hd)
            kdb_ref[hd] = jnp.exp(r * lgb[:, :DK]) * DK ** -0.5
            cdb_ref[hd] = jnp.broadcast_to(jnp.exp(float(CHUNK) * lgb), (8, DV))

    @pl.when(i == 0)
    def _():
        S_ref[...] = jnp.zeros_like(S_ref)

    chunks = [slice(ci * CHUNK, (ci + 1) * CHUNK) for ci in reversed(range(k_ref.shape[1] // CHUNK))]

    def tile(src_ref):
        ks, vs = [], []
        for rows in chunks:
            if normalize:
                h = _rms(src_ref[0, rows, :], g_ref[...]).astype(BF16)
                h_ref[0, rows, :] = h
            else:
                h = src_ref[0, rows, :]
            ks.append(_dot(h, wk_ref[...]))
            v = jnp.concatenate([_dot(h, wv0_ref[...]), _dot(h, wv1_ref[...])], axis=-1).astype(BF16)
            v_ref[0, rows, :] = v
            vs.append(v)
        for rows, k, v in zip(chunks, ks, vs):
            ci = rows.start // CHUNK
            c, s = _rotary_chunk(bt_ref, ot_ref, ci)
            for hd in range(HEADS):
                kh = _rot(k[:, hd * DK:(hd + 1) * DK], c, s)
                k_ref[0, rows, hd * DK:(hd + 1) * DK] = kh.astype(BF16)
                S = S_ref[hd]
                sb_ref[0, ci, hd] = S.astype(BF16)
                upd = _dot_tn((kh * kdb_ref[hd]).astype(BF16), v[:, hd * DV:(hd + 1) * DV])
                S_ref[hd] = S * cdb_ref[hd][0:1, :] + upd

    if normalize:
        pl.when(b == 0)(lambda: tile(xp_ref))
        pl.when(b != 0)(lambda: tile(xs_ref))
    else:
        tile(x_ref)


def _mixer_kernel(*refs, seq_len, paired):
    xs_ref = None
    if paired:
        x_ref, xs_ref, *refs = refs
    else:
        x_ref, *refs = refs
    (h_ref, hp_ref, hn_ref, k_ref, v_ref, sb_ref, bt_ref, ot_ref, dl_ref,
     win_ref, pw_ref, ps_ref, wpo_ref, wro_ref, wo_ref, o_ref,
     S_ref, mask_ref, qd_ref, kdf_ref, cdf_ref, hext_ref, xa_ref, wpool_ref, scm_ref, qdec_ref) = refs
    b = pl.program_id(0)
    i = pl.program_id(1)
    nt = pl.num_programs(1)
    T = h_ref.shape[1]
    C = CHUNK
    chunks = [slice(ci * C, (ci + 1) * C) for ci in range(T // C)]
    heads = [slice(hd * DK, (hd + 1) * DK) for hd in range(HEADS)]
    vheads = [slice(hd * DV, (hd + 1) * DV) for hd in range(HEADS)]

    @pl.when((b == 0) & (i == 0))
    def _():
        ri = lax.broadcasted_iota(jnp.int32, (C, C), 0).astype(F32)
        ci = lax.broadcasted_iota(jnp.int32, (C, C), 1).astype(F32)
        r = lax.broadcasted_iota(jnp.int32, (C, DK), 0).astype(F32)
        for hd in range(HEADS):
            lgf = _log_decay(dl_ref, FWD, hd)
            lgb = _log_decay(dl_ref, BWD, hd)
            dfw = ri - ci
            m = jnp.where(dfw >= 0.0,
                          jnp.exp(lgf[:, :C] * jnp.maximum(dfw, 0.0)),
                          jnp.exp(lgb[:, :C] * jnp.maximum(-dfw, 0.0)))
            mask_ref[hd] = m * DK ** -0.5
            qd_ref[hd, :, 0:DK] = jnp.exp((r + 1.0) * lgf[:, :DK])
            qd_ref[hd, :, DK:2 * DK] = jnp.exp((C - r) * lgb[:, :DK])
            kdf_ref[hd] = jnp.exp((C - 1.0 - r) * lgf[:, :DK]) * DK ** -0.5
            cdf_ref[hd] = jnp.broadcast_to(jnp.exp(float(C) * lgf), (8, DV))
        for gi in range(len(POOL_WINDOWS)):
            cols = slice(gi * POOL_GROUP_DIM, (gi + 1) * POOL_GROUP_DIM)
            wpool_ref[cols, :] = jnp.dot(pw_ref[gi] * ps_ref[:, cols], wpo_ref[cols, :],
                                         precision=lax.Precision.HIGHEST,
                                         preferred_element_type=F32).astype(BF16)

    @pl.when(i == 0)
    def _():
        S_ref[...] = jnp.zeros_like(S_ref)

    h = h_ref[0]
    q = _dot(h, win_ref[:, OFF_Q:OFF_K])

    hext_ref[0:HALO, :] = jnp.where(i > 0, hp_ref[0], jnp.zeros_like(hp_ref[0]))
    hext_ref[HALO:HALO + T, :] = h
    hext_ref[HALO + T:, :] = jnp.where(i < nt - 1, hn_ref[0], jnp.zeros_like(hn_ref[0]))
    xa_ref[...] = _dot(hext_ref[...], win_ref[:, 0:POOL_WIDTH])

    rot = [_rotary_chunk(bt_ref, ot_ref, ci) for ci in range(len(chunks))]
    dyn0 = jnp.minimum(i, 0)
    for ci, rows in enumerate(chunks):
        for hd in range(HEADS):
            qh = _rot(q[rows, heads[hd]], *rot[ci])
            qdec_ref[ci + dyn0, hd] = (jnp.concatenate([qh, qh], axis=-1) * qd_ref[hd]).astype(BF16)
            sc = _dot_nt(qh.astype(BF16), k_ref[0, rows, heads[hd]])
            scm_ref[ci + dyn0, hd] = (sc * mask_ref[hd]).astype(BF16)

    xg = _dot(h_ref[0], win_ref[:, OFF_G:OFF_GATE])

    ys = []
    for ci, rows in enumerate(chunks):
        ys.append([])
        for hd in range(HEADS):
            vb = v_ref[0, rows, vheads[hd]]
            S = S_ref[hd]
            states = jnp.concatenate([S.astype(BF16), sb_ref[0, ci, hd]], axis=0)
            ys[ci].append(_dot(scm_ref[ci + dyn0, hd], vb) + _dot(qdec_ref[ci + dyn0, hd], states))
            kf = (k_ref[0, rows, heads[hd]].astype(F32) * kdf_ref[hd]).astype(BF16)
            S_ref[hd] = S * cdf_ref[hd][0:1, :] + _dot_tn(kf, vb)

    gz = _dot(h_ref[0], win_ref[:, OFF_GATE:])

    tglob = i * T + lax.broadcasted_iota(jnp.int32, (T, POOL_GROUP_DIM), 0)
    diffs = []
    for gi, w in enumerate(POOL_WINDOWS):
        cols = slice(gi * POOL_GROUP_DIM, (gi + 1) * POOL_GROUP_DIM)
        acc = xa_ref[HALO - w // 2:HALO - w // 2 + T, cols]
        for kk in range(-w // 2 + 1, w // 2):
            acc = acc + xa_ref[HALO + kk:HALO + kk + T, cols]
        lo = jnp.maximum(tglob - w // 2, 0)
        hi = jnp.minimum(tglob + w // 2, seq_len)
        mean = acc / (hi - lo).astype(F32)
        diffs.append((mean - xa_ref[HALO:HALO + T, cols]).astype(BF16))
    a = _dot(jnp.concatenate(diffs, axis=-1), wpool_ref[...])

    rs = []
    for ci, rows in enumerate(chunks):
        rin = []
        for hd in range(HEADS):
            y = ys[ci][hd]
            yc = y - jnp.mean(y, axis=-1, keepdims=True)
            yn = yc * lax.rsqrt(jnp.mean(yc * yc, axis=-1, keepdims=True) + GN_EPS)
            gh = xg[rows, vheads[hd]]
            rin.append((yn * _silu_of_half(gh)).astype(BF16))
        rs.append(_dot(jnp.concatenate(rin, axis=-1), wro_ref[...]))

    for rows, r in zip(chunks, rs):
        th = jnp.tanh(gz[rows])
        merged = 0.5 * ((th[:, :D_MODEL] + 1.0) * a[rows] + (th[:, D_MODEL:] + 1.0) * r)
        x = x_ref[0, rows, :]
        if paired:
            x = jnp.where(b == 0, x, xs_ref[0, rows, :])
        o_ref[0, rows, :] = x + _dot(merged.astype(BF16), wo_ref[...])


def _ffn_kernel(*refs, final_norm):
    if final_norm:
        x_ref, p_ref, gf_ref, wfi_ref, wfo_ref, gp_ref, wpg_ref, wpp_ref, gout_ref, o_ref = refs
    else:
        (x_ref, p_ref, ps_ref, gf_ref, wfi_ref, wfo_ref, gp_ref, wpg_ref, wpp_ref, gout_ref,
         o_ref, hn_ref) = refs
    b = pl.program_id(0)
    rpg = x_ref.shape[1] // FFN_GROUPS
    groups = [slice(gi * rpg, (gi + 1) * rpg) for gi in range(FFN_GROUPS)]
    xs = [x_ref[0, rows, :] for rows in groups]
    gt, up, pp = [], [], []
    for x, rows in zip(xs, groups):
        h2 = _rms(x, gf_ref[...]).astype(BF16)
        gt.append(_dot(h2, wfi_ref[:, :D_FF]))
        up.append(_dot(h2, wfi_ref[:, D_FF:]))
        p = p_ref[0, rows, :]
        if not final_norm:
            p = jnp.where(b == 0, p, ps_ref[0, rows, :])
        pp.append(_dot(p.astype(BF16), wpp_ref[...]))
    xs = [x + _dot((_silu_of_half(g) * u).astype(BF16), wfo_ref[...])
          for x, g, u in zip(xs, gt, up)]
    pg = [_sigmoid_of_half(_dot(_rms(x, gp_ref[...]).astype(BF16), wpg_ref[...])) for x in xs]
    for x, rows, proj, gate in zip(xs, groups, pp, pg):
        x = x + proj * gate
        if final_norm:
            o_ref[0, rows, :] = _rms(x, gout_ref[...])
        else:
            o_ref[0, rows, :] = x
            hn_ref[0, rows, :] = _rms(x, gout_ref[...]).astype(BF16)


def _layer_spec(arr, layer):
    nd = arr.ndim - 1
    return pl.BlockSpec((None,) + arr.shape[1:], lambda b, i: (layer,) + (0,) * nd,
                        pipeline_mode=pl.Buffered(1))


def _const_spec(arr):
    nd = arr.ndim
    return pl.BlockSpec(arr.shape, lambda b, i: (0,) * nd, pipeline_mode=pl.Buffered(1))


def _pair_specs(block, nt, tile_of, lead=()):
    first, last = tile_of(0), tile_of(nt - 1)
    prompt = pl.BlockSpec(block, lambda b, i: lead + (0, jnp.where(b == 0, tile_of(i), last), 0))
    sample = pl.BlockSpec(block, lambda b, i: lead + (jnp.maximum(b - 1, 0),
                                                      jnp.where(b == 0, first, tile_of(i)), 0))
    return [prompt, sample]


def _params(semantics):
    return pltpu.CompilerParams(dimension_semantics=semantics, vmem_limit_bytes=VMEM_LIMIT)


def _kv(x, layer, g_mix, w_in, rot_base, rot_off, dl, normalize):
    B = x[0].shape[0] + x[1].shape[0] if normalize else x.shape[0]
    _, L, D = x[0].shape if normalize else x.shape
    T = T_KV
    nt = L // T
    cpt = T // CHUNK
    rev = lambda b, i: (b, nt - 1 - i, 0)
    wcols = lambda off: pl.BlockSpec((None, D, QK_WIDTH), lambda b, i: (layer, 0, off // QK_WIDTH),
                                     pipeline_mode=pl.Buffered(1))
    out_specs, out_shape = [], []
    if normalize:
        in_specs = _pair_specs((1, T, D), nt, lambda i: nt - 1 - i) + [_layer_spec(g_mix, layer)]
        args = [x[0], x[1], g_mix]
        out_specs.append(pl.BlockSpec((1, T, D), rev))
        out_shape.append(jax.ShapeDtypeStruct((B, L, D), BF16))
    else:
        in_specs = [pl.BlockSpec((1, T, D), rev)]
        args = [x]
    in_specs += [
        wcols(OFF_K),
        wcols(OFF_V),
        wcols(OFF_V + QK_WIDTH),
        pl.BlockSpec((cpt, 3, DK), lambda b, i: (nt - 1 - i, 0, 0)),
        _const_spec(rot_off),
        _layer_spec(dl, layer),
    ]
    args += [w_in, w_in, w_in, rot_base, rot_off, dl]
    out_specs += [
        pl.BlockSpec((1, T, QK_WIDTH), rev),
        pl.BlockSpec((1, T, V_WIDTH), rev),
        pl.BlockSpec((1, cpt, HEADS, DK, DV), lambda b, i: (b, nt - 1 - i, 0, 0, 0)),
    ]
    out_shape += [
        jax.ShapeDtypeStruct((B, L, QK_WIDTH), BF16),
        jax.ShapeDtypeStruct((B, L, V_WIDTH), BF16),
        jax.ShapeDtypeStruct((B, L // CHUNK, HEADS, DK, DV), BF16),
    ]
    return pl.pallas_call(
        functools.partial(_kv_kernel, normalize=normalize),
        grid=(B, nt),
        in_specs=in_specs,
        out_specs=out_specs,
        out_shape=out_shape,
        scratch_shapes=[
            pltpu.VMEM((HEADS, DK, DV), F32),
            pltpu.VMEM((HEADS, CHUNK, DK), F32),
            pltpu.VMEM((HEADS, 8, DV), F32),
        ],
        compiler_params=_params(("arbitrary", "arbitrary")),
        name="kv",
    )(*args)


def _mixer(x, h, k, v, sb, layer, w_in, pool_w, pool_scale, w_pool_out, w_ret_out, w_o,
           rot_base, rot_off, dl):
    paired = isinstance(x, tuple)
    B, L, D = h.shape
    T = T_MIX
    nt = L // T
    cpt = T // CHUNK
    hb = T // HALO
    nhb = L // HALO
    tile = lambda b, i: (b, i, 0)
    return pl.pallas_call(
        functools.partial(_mixer_kernel, seq_len=L, paired=paired),
        grid=(B, nt),
        in_specs=(_pair_specs((1, T, D), nt, lambda i: i) if paired
                  else [pl.BlockSpec((1, T, D), tile)]) + [
            pl.BlockSpec((1, T, D), tile),
            pl.BlockSpec((1, HALO, D), lambda b, i: (b, jnp.maximum(i * hb - 1, 0), 0)),
            pl.BlockSpec((1, HALO, D), lambda b, i: (b, jnp.minimum((i + 1) * hb, nhb - 1), 0)),
            pl.BlockSpec((1, T, QK_WIDTH), tile),
            pl.BlockSpec((1, T, V_WIDTH), tile),
            pl.BlockSpec((1, cpt, HEADS, DK, DV), lambda b, i: (b, i, 0, 0, 0)),
            pl.BlockSpec((cpt, 3, DK), lambda b, i: (i, 0, 0)),
            _const_spec(rot_off),
            _layer_spec(dl, layer),
            _layer_spec(w_in, layer),
            _layer_spec(pool_w, layer),
            _layer_spec(pool_scale, layer),
            _layer_spec(w_pool_out, layer),
            _layer_spec(w_ret_out, layer),
            _layer_spec(w_o, layer),
        ],
        out_specs=pl.BlockSpec((1, T, D), tile),
        out_shape=jax.ShapeDtypeStruct((B, L, D), F32),
        scratch_shapes=[
            pltpu.VMEM((HEADS, DK, DV), F32),
            pltpu.VMEM((HEADS, CHUNK, CHUNK), F32),
            pltpu.VMEM((HEADS, CHUNK, 2 * DK), F32),
            pltpu.VMEM((HEADS, CHUNK, DK), F32),
            pltpu.VMEM((HEADS, 8, DV), F32),
            pltpu.VMEM((T + 2 * HALO, D), BF16),
            pltpu.VMEM((T + 2 * HALO, POOL_WIDTH), F32),
            pltpu.VMEM((POOL_WIDTH, D), BF16),
            pltpu.VMEM((cpt, HEADS, CHUNK, CHUNK), BF16),
            pltpu.VMEM((cpt, HEADS, CHUNK, 2 * DK), BF16),
        ],
        compiler_params=_params(("arbitrary", "arbitrary")),
        name="mixer",
    )(*(x if paired else (x,)), h, h, h, k, v, sb, rot_base, rot_off, dl, w_in, pool_w, pool_scale,
      w_pool_out, w_ret_out, w_o)


def _ffn(x, p, layer, g_ffn, w_ffn_in, w_ffn_out, g_ple, w_ple_gate, w_ple_proj, g_out, final_norm,
         batch_offset=0):
    _, L, D = x.shape
    T = T_FFN
    nt = L // T
    tile = lambda b, i: (b, i, 0)
    if final_norm:
        B = p.shape[1]
        p_specs = [pl.BlockSpec((None, 1, T, PLE_DIM), lambda b, i: (layer, b, i, 0))]
        p_args = [p]
        out_specs = pl.BlockSpec((1, T, D), tile)
        out_shape = jax.ShapeDtypeStruct((B, L, D), F32)
    else:
        B = x.shape[0]
        p_specs = _pair_specs((None, 1, T, PLE_DIM), nt, lambda i: i, lead=(layer,))
        p_args = list(p)
        out_specs = [pl.BlockSpec((1, T, D), tile)] * 2
        out_shape = [jax.ShapeDtypeStruct((B, L, D), F32), jax.ShapeDtypeStruct((B, L, D), BF16)]
    return pl.pallas_call(
        functools.partial(_ffn_kernel, final_norm=final_norm),
        grid=(B, nt),
        in_specs=[pl.BlockSpec((1, T, D), lambda b, i: (b + batch_offset, i, 0))] + p_specs + [
            _layer_spec(g_ffn, layer),
            _layer_spec(w_ffn_in, layer),
            _layer_spec(w_ffn_out, layer),
            _layer_spec(g_ple, layer),
            _layer_spec(w_ple_gate, layer),
            _layer_spec(w_ple_proj, layer),
            _const_spec(g_out),
        ],
        out_specs=out_specs,
        out_shape=out_shape,
        compiler_params=_params(("arbitrary", "arbitrary")),
        name="ffn",
    )(x, *p_args, g_ffn, w_ffn_in, w_ffn_out, g_ple, w_ple_gate, w_ple_proj, g_out)


def _rotary_tables(L):
    half = DK // 2
    inv = 1.0 / (ROPE_BASE ** (jnp.arange(half, dtype=F32) / half))
    base = (jnp.arange(L // CHUNK) * CHUNK).astype(F32)[:, None] * inv
    off = jnp.arange(CHUNK).astype(F32)[:, None] * inv
    planes = lambda c, s: jnp.stack([jnp.concatenate([c, c], -1), jnp.concatenate([s, s], -1),
                                     jnp.concatenate([-s, s], -1)])
    rot_base = jnp.transpose(planes(jnp.cos(base), jnp.sin(base)), (1, 0, 2))
    return rot_base, planes(jnp.cos(off), jnp.sin(off))


def kernel(x_prompt, x_sample, p_prompt, p_sample, g_mix, w_in, pool_w, pool_scale, decay_logit,
           w_pool_out, w_ret_out, w_o, g_ffn, w_ffn_in, w_ffn_out, g_ple, w_ple_gate, w_ple_proj,
           g_final):
    depth = w_in.shape[0]
    L = x_prompt.shape[1]
    rot_base, rot_off = _rotary_tables(L)
    halved = lambda width, lo, hi: jnp.where((jnp.arange(width) >= lo) & (jnp.arange(width) < hi),
                                             HALF, 1.0).astype(F32)
    w_in_b = (w_in * halved(w_in.shape[-1], OFF_G, w_in.shape[-1])).astype(BF16)
    w_ret_out_b = w_ret_out.astype(BF16)
    w_o_b = w_o.astype(BF16)
    w_ffn_in_b = (w_ffn_in * halved(2 * D_FF, 0, D_FF)).astype(BF16)
    w_ffn_out_b = w_ffn_out.astype(BF16)
    w_ple_gate_b = (w_ple_gate * HALF).astype(BF16)
    w_ple_proj_b = w_ple_proj.astype(BF16)
    dl = jnp.broadcast_to(decay_logit.astype(F32)[..., None, None], decay_logit.shape + (1, DV))
    rows = lambda v: v.reshape(v.shape[0], 1, v.shape[1])
    g_mix_r, g_ffn_r, g_ple_r, pool_scale_r = rows(g_mix), rows(g_ffn), rows(g_ple), rows(pool_scale)
    g_final_r = g_final.reshape(1, -1)

    x = (x_prompt, x_sample)
    p = (p_prompt, p_sample)
    ffn_w = (g_ffn_r, w_ffn_in_b, w_ffn_out_b, g_ple_r, w_ple_gate_b, w_ple_proj_b)
    for l in range(depth):
        if l == 0:
            h, k, v, sb = _kv(x, l, g_mix_r, w_in_b, rot_base, rot_off, dl, True)
        else:
            k, v, sb = _kv(h, l, g_mix_r, w_in_b, rot_base, rot_off, dl, False)
        x = _mixer(x, h, k, v, sb, l, w_in_b, pool_w, pool_scale_r, w_pool_out,
                   w_ret_out_b, w_o_b, rot_base, rot_off, dl)
        if l < depth - 1:
            x, h = _ffn(x, p, l, *ffn_w, g_mix_r[l + 1], False)
    nb = x_prompt.shape[0]
    return (_ffn(x, p_prompt, depth - 1, *ffn_w, g_final_r, True),
            _ffn(x, p_sample, depth - 1, *ffn_w, g_final_r, True, batch_offset=nb))
```

```python
import functools

import jax
import jax.numpy as jnp
from jax import lax
from jax.experimental import pallas as pl
from jax.experimental.pallas import tpu as pltpu

F32 = jnp.float32
BF16 = jnp.bfloat16

D_MODEL = 1024
PLE_DIM = 256
POOL_GROUP_DIM = 128
POOL_WIDTH = 512
POOL_WINDOWS = (2, 4, 8, 16)
HEADS = 4
DK = 128
DV = 256
QK_WIDTH = HEADS * DK
V_WIDTH = HEADS * DV
D_FF = 2816
ROPE_BASE = 10000.0
RMS_EPS = 1e-6
GN_EPS = 1e-5

OFF_Q = POOL_WIDTH
OFF_K = OFF_Q + QK_WIDTH
OFF_V = OFF_K + QK_WIDTH
OFF_G = OFF_V + V_WIDTH
OFF_GATE = OFF_G + V_WIDTH

HALO = 16
CHUNK = 256
T_KV = 1024
T_MIX = 512
T_FFN = 1024
FFN_GROUPS = 4
VMEM_LIMIT = 60 * 1024 * 1024
FWD, BWD = 0, 1


def _rms(x, g):
    return x * lax.rsqrt(jnp.mean(x * x, axis=-1, keepdims=True) + RMS_EPS) * g


HALF = 0.5


def _sigmoid_of_half(t):
    return 0.5 * jnp.tanh(t) + 0.5


def _silu_of_half(t):
    return t * jnp.tanh(t) + t


def _rot(x, c, s):
    return x * c + pltpu.roll(x, DK // 2, 1) * s


def _rotary_chunk(bt_ref, ot_ref, ci):
    cb, sb, sbs = bt_ref[ci, 0:1, :], bt_ref[ci, 1:2, :], bt_ref[ci, 2:3, :]
    co, so, sos = ot_ref[0], ot_ref[1], ot_ref[2]
    return cb * co - sb * so, sbs * co + cb * sos


def _dot(a, b):
    return jnp.dot(a, b, preferred_element_type=F32)


def _dot_tn(a, b):
    return lax.dot_general(a, b, (((0,), (0,)), ((), ())), preferred_element_type=F32)


def _dot_nt(a, b):
    return lax.dot_general(a, b, (((1,), (1,)), ((), ())), preferred_element_type=F32)


def _log_decay(dl_ref, direction, h):
    x = dl_ref[direction, h]
    return jnp.minimum(x, 0.0) - jnp.log1p(jnp.exp(-jnp.abs(x)))


def _kv_kernel(*refs, normalize):
    if normalize:
        (xp_ref, xs_ref, g_ref, wk_ref, wv0_ref, wv1_ref, bt_ref, ot_ref, dl_ref,
         h_ref, k_ref, v_ref, sb_ref, S_ref, kdb_ref, cdb_ref) = refs
    else:
        (x_ref, wk_ref, wv0_ref, wv1_ref, bt_ref, ot_ref, dl_ref,
         k_ref, v_ref, sb_ref, S_ref, kdb_ref, cdb_ref) = refs
    b = pl.program_id(0)
    i = pl.program_id(1)

    @pl.when((b == 0) & (i == 0))
    def _():
        r = lax.broadcasted_iota(jnp.int32, (CHUNK, DK), 0).astype(F32)
        for hd in range(HEADS):
            lgb = _log_decay(dl_ref, BWD, hd)
            kdb_ref[hd] = jnp.exp(r * lgb[:, :DK]) * DK ** -0.5
            cdb_ref[hd] = jnp.broadcast_to(jnp.exp(float(CHUNK) * lgb), (8, DV))

    @pl.when(i == 0)
    def _():
        S_ref[...] = jnp.zeros_like(S_ref)

    chunks = [slice(ci * CHUNK, (ci + 1) * CHUNK) for ci in reversed(range(k_ref.shape[1] // CHUNK))]

    def tile(src_ref):
        ks, vs = [], []
        for rows in chunks:
            if normalize:
                h = _rms(src_ref[0, rows, :], g_ref[...]).astype(BF16)
                h_ref[0, rows, :] = h
            else:
                h = src_ref[0, rows, :]
            ks.append(_dot(h, wk_ref[...]))
            v = jnp.concatenate([_dot(h, wv0_ref[...]), _dot(h, wv1_ref[...])], axis=-1).astype(BF16)
            v_ref[0, rows, :] = v
            vs.append(v)
        for rows, k, v in zip(chunks, ks, vs):
            ci = rows.start // CHUNK
            c, s = _rotary_chunk(bt_ref, ot_ref, ci)
            for hd in range(HEADS):
                kh = _rot(k[:, hd * DK:(hd + 1) * DK], c, s)
                k_ref[0, rows, hd * DK:(hd + 1) * DK] = kh.astype(BF16)
                S = S_ref[hd]
                sb_ref[0, ci, hd] = S.astype(BF16)
                upd = _dot_tn((kh * kdb_ref[hd]).astype(BF16), v[:, hd * DV:(hd + 1) * DV])
                S_ref[hd] = S * cdb_ref[hd][0:1, :] + upd

    if normalize:
        pl.when(b == 0)(lambda: tile(xp_ref))
        pl.when(b != 0)(lambda: tile(xs_ref))
    else:
        tile(x_ref)


def _mixer_kernel(*refs, seq_len, paired):
    xs_ref = None
    if paired:
        x_ref, xs_ref, *refs = refs
    else:
        x_ref, *refs = refs
    (h_ref, hp_ref, hn_ref, k_ref, v_ref, sb_ref, bt_ref, ot_ref, dl_ref,
     win_ref, pw_ref, ps_ref, wpo_ref, wro_ref, wo_ref, o_ref,
     S_ref, mask_ref, qd_ref, kdf_ref, cdf_ref, hext_ref, xa_ref, wpool_ref, scm_ref, qdec_ref) = refs
    b = pl.program_id(0)
    i = pl.program_id(1)
    nt = pl.num_programs(1)
    T = h_ref.shape[1]
    C = CHUNK
    chunks = [slice(ci * C, (ci + 1) * C) for ci in range(T // C)]
    heads = [slice(hd * DK, (hd + 1) * DK) for hd in range(HEADS)]
    vheads = [slice(hd * DV, (hd + 1) * DV) for hd in range(HEADS)]

    @pl.when((b == 0) & (i == 0))
    def _():
        ri = lax.broadcasted_iota(jnp.int32, (C, C), 0).astype(F32)
        ci = lax.broadcasted_iota(jnp.int32, (C, C), 1).astype(F32)
        r = lax.broadcasted_iota(jnp.int32, (C, DK), 0).astype(F32)
        for hd in range(HEADS):
            lgf = _log_decay(dl_ref, FWD, hd)
            lgb = _log_decay(dl_ref, BWD, hd)
            dfw = ri - ci
            m = jnp.where(dfw >= 0.0,
                          jnp.exp(lgf[:, :C] * jnp.maximum(dfw, 0.0)),
                          jnp.exp(lgb[:, :C] * jnp.maximum(-dfw, 0.0)))
            mask_ref[hd] = m * DK ** -0.5
            qd_ref[hd, :, 0:DK] = jnp.exp((r + 1.0) * lgf[:, :DK])
            qd_ref[hd, :, DK:2 * DK] = jnp.exp((C - r) * lgb[:, :DK])
            kdf_ref[hd] = jnp.exp((C - 1.0 - r) * lgf[:, :DK]) * DK ** -0.5
            cdf_ref[hd] = jnp.broadcast_to(jnp.exp(float(C) * lgf), (8, DV))
        for gi in range(len(POOL_WINDOWS)):
            cols = slice(gi * POOL_GROUP_DIM, (gi + 1) * POOL_GROUP_DIM)
            wpool_ref[cols, :] = jnp.dot(pw_ref[gi] * ps_ref[:, cols], wpo_ref[cols, :],
                                         precision=lax.Precision.HIGHEST,
                                         preferred_element_type=F32).astype(BF16)

    @pl.when(i == 0)
    def _():
        S_ref[...] = jnp.zeros_like(S_ref)

    h = h_ref[0]
    q = _dot(h, win_ref[:, OFF_Q:OFF_K])

    hext_ref[0:HALO, :] = jnp.where(i > 0, hp_ref[0], jnp.zeros_like(hp_ref[0]))
    hext_ref[HALO:HALO + T, :] = h
    hext_ref[HALO + T:, :] = jnp.where(i < nt - 1, hn_ref[0], jnp.zeros_like(hn_ref[0]))
    xa_ref[...] = _dot(hext_ref[...], win_ref[:, 0:POOL_WIDTH])

    rot = [_rotary_chunk(bt_ref, ot_ref, ci) for ci in range(len(chunks))]
    dyn0 = jnp.minimum(i, 0)
    for ci, rows in enumerate(chunks):
        for hd in range(HEADS):
            qh = _rot(q[rows, heads[hd]], *rot[ci])
            qdec_ref[ci + dyn0, hd] = (jnp.concatenate([qh, qh], axis=-1) * qd_ref[hd]).astype(BF16)
            sc = _dot_nt(qh.astype(BF16), k_ref[0, rows, heads[hd]])
            scm_ref[ci + dyn0, hd] = (sc * mask_ref[hd]).astype(BF16)

    xg = _dot(h_ref[0], win_ref[:, OFF_G:OFF_GATE])

    ys = []
    for ci, rows in enumerate(chunks):
        ys.append([])
        for hd in range(HEADS):
            vb = v_ref[0, rows, vheads[hd]]
            S = S_ref[hd]
            states = jnp.concatenate([S.astype(BF16), sb_ref[0, ci, hd]], axis=0)
            ys[ci].append(_dot(scm_ref[ci + dyn0, hd], vb) + _dot(qdec_ref[ci + dyn0, hd], states))
            kf = (k_ref[0, rows, heads[hd]].astype(F32) * kdf_ref[hd]).astype(BF16)
            S_ref[hd] = S * cdf_ref[hd][0:1, :] + _dot_tn(kf, vb)

    gz = _dot(h_ref[0], win_ref[:, OFF_GATE:])

    tglob = i * T + lax.broadcasted_iota(jnp.int32, (T, POOL_GROUP_DIM), 0)
    diffs = []
    for gi, w in enumerate(POOL_WINDOWS):
        cols = slice(gi * POOL_GROUP_DIM, (gi + 1) * POOL_GROUP_DIM)
        if w <= 4:
            acc = xa_ref[HALO - w // 2:HALO - w // 2 + T, cols]
            for kk in range(-w // 2 + 1, w // 2):
                acc = acc + xa_ref[HALO + kk:HALO + kk + T, cols]
        else:
            n = T + HALO
            part = xa_ref[HALO // 2:HALO // 2 + n, cols]
            for k in (1, 2, 4, 8)[:w.bit_length() - 1]:
                part = part + pltpu.roll(part, n - k, 0)
            shift = HALO // 2 - w // 2
            acc = (pltpu.roll(part, n - shift, 0) if shift else part)[0:T]
        lo = jnp.maximum(tglob - w // 2, 0)
        hi = jnp.minimum(tglob + w // 2, seq_len)
        mean = acc / (hi - lo).astype(F32)
        diffs.append((mean - xa_ref[HALO:HALO + T, cols]).astype(BF16))
    a = _dot(jnp.concatenate(diffs, axis=-1), wpool_ref[...])

    rs = []
    for ci, rows in enumerate(chunks):
        rin = []
        for hd in range(HEADS):
            y = ys[ci][hd]
            yc = y - jnp.mean(y, axis=-1, keepdims=True)
            yn = yc * lax.rsqrt(jnp.mean(yc * yc, axis=-1, keepdims=True) + GN_EPS)
            gh = xg[rows, vheads[hd]]
            rin.append((yn * _silu_of_half(gh)).astype(BF16))
        rs.append(_dot(jnp.concatenate(rin, axis=-1), wro_ref[...]))

    for rows, r in zip(chunks, rs):
        th = jnp.tanh(gz[rows])
        merged = 0.5 * ((th[:, :D_MODEL] + 1.0) * a[rows] + (th[:, D_MODEL:] + 1.0) * r)
        x = x_ref[0, rows, :]
        if paired:
            x = jnp.where(b == 0, x, xs_ref[0, rows, :])
        o_ref[0, rows, :] = x + _dot(merged.astype(BF16), wo_ref[...])


def _ffn_kernel(*refs, final_norm):
    if final_norm:
        x_ref, p_ref, gf_ref, wfi_ref, wfo_ref, gp_ref, wpg_ref, wpp_ref, gout_ref, o_ref = refs
    else:
        (x_ref, p_ref, ps_ref, gf_ref, wfi_ref, wfo_ref, gp_ref, wpg_ref, wpp_ref, gout_ref,
         o_ref, hn_ref) = refs
    b = pl.program_id(0)
    rpg = x_ref.shape[1] // FFN_GROUPS
    groups = [slice(gi * rpg, (gi + 1) * rpg) for gi in range(FFN_GROUPS)]
    xs = [x_ref[0, rows, :] for rows in groups]
    gt, up, pp = [], [], []
    for x, rows in zip(xs, groups):
        h2 = _rms(x, gf_ref[...]).astype(BF16)
        gt.append(_dot(h2, wfi_ref[:, :D_FF]))
        up.append(_dot(h2, wfi_ref[:, D_FF:]))
        p = p_ref[0, rows, :]
        if not final_norm:
            p = jnp.where(b == 0, p, ps_ref[0, rows, :])
        pp.append(_dot(p.astype(BF16), wpp_ref[...]))
    xs = [x + _dot((_silu_of_half(g) * u).astype(BF16), wfo_ref[...])
          for x, g, u in zip(xs, gt, up)]
    pg = [_sigmoid_of_half(_dot(_rms(x, gp_ref[...]).astype(BF16), wpg_ref[...])) for x in xs]
    for x, rows, proj, gate in zip(xs, groups, pp, pg):
        x = x + proj * gate
        if final_norm:
            o_ref[0, rows, :] = _rms(x, gout_ref[...])
        else:
            o_ref[0, rows, :] = x
            hn_ref[0, rows, :] = _rms(x, gout_ref[...]).astype(BF16)


def _layer_spec(arr, layer):
    nd = arr.ndim - 1
    return pl.BlockSpec((None,) + arr.shape[1:], lambda b, i: (layer,) + (0,) * nd,
                        pipeline_mode=pl.Buffered(1))


def _const_spec(arr):
    nd = arr.ndim
    return pl.BlockSpec(arr.shape, lambda b, i: (0,) * nd, pipeline_mode=pl.Buffered(1))


def _pair_specs(block, nt, tile_of, lead=()):
    first, last = tile_of(0), tile_of(nt - 1)
    prompt = pl.BlockSpec(block, lambda b, i: lead + (0, jnp.where(b == 0, tile_of(i), last), 0))
    sample = pl.BlockSpec(block, lambda b, i: lead + (jnp.maximum(b - 1, 0),
                                                      jnp.where(b == 0, first, tile_of(i)), 0))
    return [prompt, sample]


def _params(semantics):
    return pltpu.CompilerParams(dimension_semantics=semantics, vmem_limit_bytes=VMEM_LIMIT)


def _kv(x, layer, g_mix, w_in, rot_base, rot_off, dl, normalize):
    B = x[0].shape[0] + x[1].shape[0] if normalize else x.shape[0]
    _, L, D = x[0].shape if normalize else x.shape
    T = T_KV
    nt = L // T
    cpt = T // CHUNK
    rev = lambda b, i: (b, nt - 1 - i, 0)
    wcols = lambda off: pl.BlockSpec((None, D, QK_WIDTH), lambda b, i: (layer, 0, off // QK_WIDTH),
                                     pipeline_mode=pl.Buffered(1))
    out_specs, out_shape = [], []
    if normalize:
        in_specs = _pair_specs((1, T, D), nt, lambda i: nt - 1 - i) + [_layer_spec(g_mix, layer)]
        args = [x[0], x[1], g_mix]
        out_specs.append(pl.BlockSpec((1, T, D), rev))
        out_shape.append(jax.ShapeDtypeStruct((B, L, D), BF16))
    else:
        in_specs = [pl.BlockSpec((1, T, D), rev)]
        args = [x]
    in_specs += [
        wcols(OFF_K),
        wcols(OFF_V),
        wcols(OFF_V + QK_WIDTH),
        pl.BlockSpec((cpt, 3, DK), lambda b, i: (nt - 1 - i, 0, 0)),
        _const_spec(rot_off),
        _layer_spec(dl, layer),
    ]
    args += [w_in, w_in, w_in, rot_base, rot_off, dl]
    out_specs += [
        pl.BlockSpec((1, T, QK_WIDTH), rev),
        pl.BlockSpec((1, T, V_WIDTH), rev),
        pl.BlockSpec((1, cpt, HEADS, DK, DV), lambda b, i: (b, nt - 1 - i, 0, 0, 0)),
    ]
    out_shape += [
        jax.ShapeDtypeStruct((B, L, QK_WIDTH), BF16),
        jax.ShapeDtypeStruct((B, L, V_WIDTH), BF16),
        jax.ShapeDtypeStruct((B, L // CHUNK, HEADS, DK, DV), BF16),
    ]
    return pl.pallas_call(
        functools.partial(_kv_kernel, normalize=normalize),
        grid=(B, nt),
        in_specs=in_specs,
        out_specs=out_specs,
        out_shape=out_shape,
        scratch_shapes=[
            pltpu.VMEM((HEADS, DK, DV), F32),
            pltpu.VMEM((HEADS, CHUNK, DK), F32),
            pltpu.VMEM((HEADS, 8, DV), F32),
        ],
        compiler_params=_params(("arbitrary", "arbitrary")),
        name="kv",
    )(*args)


def _mixer(x, h, k, v, sb, layer, w_in, pool_w, pool_scale, w_pool_out, w_ret_out, w_o,
           rot_base, rot_off, dl):
    paired = isinstance(x, tuple)
    B, L, D = h.shape
    T = T_MIX
    nt = L // T
    cpt = T // CHUNK
    hb = T // HALO
    nhb = L // HALO
    tile = lambda b, i: (b, i, 0)
    return pl.pallas_call(
        functools.partial(_mixer_kernel, seq_len=L, paired=paired),
        grid=(B, nt),
        in_specs=(_pair_specs((1, T, D), nt, lambda i: i) if paired
                  else [pl.BlockSpec((1, T, D), tile)]) + [
            pl.BlockSpec((1, T, D), tile),
            pl.BlockSpec((1, HALO, D), lambda b, i: (b, jnp.maximum(i * hb - 1, 0), 0)),
            pl.BlockSpec((1, HALO, D), lambda b, i: (b, jnp.minimum((i + 1) * hb, nhb - 1), 0)),
            pl.BlockSpec((1, T, QK_WIDTH), tile),
            pl.BlockSpec((1, T, V_WIDTH), tile),
            pl.BlockSpec((1, cpt, HEADS, DK, DV), lambda b, i: (b, i, 0, 0, 0)),
            pl.BlockSpec((cpt, 3, DK), lambda b, i: (i, 0, 0)),
            _const_spec(rot_off),
            _layer_spec(dl, layer),
            _layer_spec(w_in, layer),
            _layer_spec(pool_w, layer),
            _layer_spec(pool_scale, layer),
            _layer_spec(w_pool_out, layer),
            _layer_spec(w_ret_out, layer),
            _layer_spec(w_o, layer),
        ],
        out_specs=pl.BlockSpec((1, T, D), tile),
        out_shape=jax.ShapeDtypeStruct((B, L, D), F32),
        scratch_shapes=[
            pltpu.VMEM((HEADS, DK, DV), F32),
            pltpu.VMEM((HEADS, CHUNK, CHUNK), F32),
            pltpu.VMEM((HEADS, CHUNK, 2 * DK), F32),
            pltpu.VMEM((HEADS, CHUNK, DK), F32),
            pltpu.VMEM((HEADS, 8, DV), F32),
            pltpu.VMEM((T + 2 * HALO, D), BF16),
            pltpu.VMEM((T + 2 * HALO, POOL_WIDTH), F32),
            pltpu.VMEM((POOL_WIDTH, D), BF16),
            pltpu.VMEM((cpt, HEADS, CHUNK, CHUNK), BF16),
            pltpu.VMEM((cpt, HEADS, CHUNK, 2 * DK), BF16),
        ],
        compiler_params=_params(("arbitrary", "arbitrary")),
        name="mixer",
    )(*(x if paired else (x,)), h, h, h, k, v, sb, rot_base, rot_off, dl, w_in, pool_w, pool_scale,
      w_pool_out, w_ret_out, w_o)


def _ffn(x, p, layer, g_ffn, w_ffn_in, w_ffn_out, g_ple, w_ple_gate, w_ple_proj, g_out, final_norm,
         batch_offset=0):
    _, L, D = x.shape
    T = T_FFN
    nt = L // T
    tile = lambda b, i: (b, i, 0)
    if final_norm:
        B = p.shape[1]
        p_specs = [pl.BlockSpec((None, 1, T, PLE_DIM), lambda b, i: (layer, b, i, 0))]
        p_args = [p]
        out_specs = pl.BlockSpec((1, T, D), tile)
        out_shape = jax.ShapeDtypeStruct((B, L, D), F32)
    else:
        B = x.shape[0]
        p_specs = _pair_specs((None, 1, T, PLE_DIM), nt, lambda i: i, lead=(layer,))
        p_args = list(p)
        out_specs = [pl.BlockSpec((1, T, D), tile)] * 2
        out_shape = [jax.ShapeDtypeStruct((B, L, D), F32), jax.ShapeDtypeStruct((B, L, D), BF16)]
    return pl.pallas_call(
        functools.partial(_ffn_kernel, final_norm=final_norm),
        grid=(B, nt),
        in_specs=[pl.BlockSpec((1, T, D), lambda b, i: (b + batch_offset, i, 0))] + p_specs + [
            _layer_spec(g_ffn, layer),
            _layer_spec(w_ffn_in, layer),
            _layer_spec(w_ffn_out, layer),
            _layer_spec(g_ple, layer),
            _layer_spec(w_ple_gate, layer),
            _layer_spec(w_ple_proj, layer),
            _const_spec(g_out),
        ],
        out_specs=out_specs,
        out_shape=out_shape,
        compiler_params=_params(("arbitrary", "arbitrary")),
        name="ffn",
    )(x, *p_args, g_ffn, w_ffn_in, w_ffn_out, g_ple, w_ple_gate, w_ple_proj, g_out)


def _rotary_tables(L):
    half = DK // 2
    inv = 1.0 / (ROPE_BASE ** (jnp.arange(half, dtype=F32) / half))
    base = (jnp.arange(L // CHUNK) * CHUNK).astype(F32)[:, None] * inv
    off = jnp.arange(CHUNK).astype(F32)[:, None] * inv
    planes = lambda c, s: jnp.stack([jnp.concatenate([c, c], -1), jnp.concatenate([s, s], -1),
                                     jnp.concatenate([-s, s], -1)])
    rot_base = jnp.transpose(planes(jnp.cos(base), jnp.sin(base)), (1, 0, 2))
    return rot_base, planes(jnp.cos(off), jnp.sin(off))


def kernel(x_prompt, x_sample, p_prompt, p_sample, g_mix, w_in, pool_w, pool_scale, decay_logit,
           w_pool_out, w_ret_out, w_o, g_ffn, w_ffn_in, w_ffn_out, g_ple, w_ple_gate, w_ple_proj,
           g_final):
    depth = w_in.shape[0]
    L = x_prompt.shape[1]
    rot_base, rot_off = _rotary_tables(L)
    halved = lambda width, lo, hi: jnp.where((jnp.arange(width) >= lo) & (jnp.arange(width) < hi),
                                             HALF, 1.0).astype(F32)
    w_in_b = (w_in * halved(w_in.shape[-1], OFF_G, w_in.shape[-1])).astype(BF16)
    w_ret_out_b = w_ret_out.astype(BF16)
    w_o_b = w_o.astype(BF16)
    w_ffn_in_b = (w_ffn_in * halved(2 * D_FF, 0, D_FF)).astype(BF16)
    w_ffn_out_b = w_ffn_out.astype(BF16)
    w_ple_gate_b = (w_ple_gate * HALF).astype(BF16)
    w_ple_proj_b = w_ple_proj.astype(BF16)
    dl = jnp.broadcast_to(decay_logit.astype(F32)[..., None, None], decay_logit.shape + (1, DV))
    rows = lambda v: v.reshape(v.shape[0], 1, v.shape[1])
    g_mix_r, g_ffn_r, g_ple_r, pool_scale_r = rows(g_mix), rows(g_ffn), rows(g_ple), rows(pool_scale)
    g_final_r = g_final.reshape(1, -1)

    x = (x_prompt, x_sample)
    p = (p_prompt, p_sample)
    ffn_w = (g_ffn_r, w_ffn_in_b, w_ffn_out_b, g_ple_r, w_ple_gate_b, w_ple_proj_b)
    for l in range(depth):
        if l == 0:
            h, k, v, sb = _kv(x, l, g_mix_r, w_in_b, rot_base, rot_off, dl, True)
        else:
            k, v, sb = _kv(h, l, g_mix_r, w_in_b, rot_base, rot_off, dl, False)
        x = _mixer(x, h, k, v, sb, l, w_in_b, pool_w, pool_scale_r, w_pool_out,
                   w_ret_out_b, w_o_b, rot_base, rot_off, dl)
        if l < depth - 1:
            x, h = _ffn(x, p, l, *ffn_w, g_mix_r[l + 1], False)
    nb = x_prompt.shape[0]
    return (_ffn(x, p_prompt, depth - 1, *ffn_w, g_final_r, True),
            _ffn(x, p_sample, depth - 1, *ffn_w, g_final_r, True, batch_offset=nb))
```

```python
import functools

import jax
import jax.numpy as jnp
from jax import lax
from jax.experimental import pallas as pl
from jax.experimental.pallas import tpu as pltpu

F32 = jnp.float32
BF16 = jnp.bfloat16

D_MODEL = 1024
PLE_DIM = 256
POOL_GROUP_DIM = 128
POOL_WIDTH = 512
POOL_WINDOWS = (2, 4, 8, 16)
HEADS = 4
DK = 128
DV = 256
QK_WIDTH = HEADS * DK
V_WIDTH = HEADS * DV
D_FF = 2816
ROPE_BASE = 10000.0
RMS_EPS = 1e-6
GN_EPS = 1e-5

OFF_Q = POOL_WIDTH
OFF_K = OFF_Q + QK_WIDTH
OFF_V = OFF_K + QK_WIDTH
OFF_G = OFF_V + V_WIDTH
OFF_GATE = OFF_G + V_WIDTH

HALO = 16
CHUNK = 256
T_KV = 1024
T_MIX = 512
T_FFN = 1024
FFN_GROUPS = 4
VMEM_LIMIT = 60 * 1024 * 1024
FWD, BWD = 0, 1


def _rms(x, g):
    return x * lax.rsqrt(jnp.mean(x * x, axis=-1, keepdims=True) + RMS_EPS) * g


HALF = 0.5


def _sigmoid_of_half(t):
    return 0.5 * jnp.tanh(t) + 0.5


def _silu_of_half(t):
    return t * jnp.tanh(t) + t


def _rot(x, c, s):
    return x * c + pltpu.roll(x, DK // 2, 1) * s


def _rotary_chunk(bt_ref, ot_ref, ci):
    cb, sb, sbs = bt_ref[ci, 0:1, :], bt_ref[ci, 1:2, :], bt_ref[ci, 2:3, :]
    co, so, sos = ot_ref[0], ot_ref[1], ot_ref[2]
    return cb * co - sb * so, sbs * co + cb * sos


def _dot(a, b):
    return jnp.dot(a, b, preferred_element_type=F32)


def _dot_tn(a, b):
    return lax.dot_general(a, b, (((0,), (0,)), ((), ())), preferred_element_type=F32)


def _dot_nt(a, b):
    return lax.dot_general(a, b, (((1,), (1,)), ((), ())), preferred_element_type=F32)


def _log_decay(dl_ref, direction, h):
    x = dl_ref[direction, h]
    return jnp.minimum(x, 0.0) - jnp.log1p(jnp.exp(-jnp.abs(x)))


def _kv_kernel(*refs, normalize):
    if normalize:
        (xp_ref, xs_ref, g_ref, wk_ref, wv0_ref, wv1_ref, bt_ref, ot_ref, dl_ref,
         h_ref, k_ref, v_ref, sb_ref, S_ref, kdb_ref, cdb_ref) = refs
    else:
        (x_ref, wk_ref, wv0_ref, wv1_ref, bt_ref, ot_ref, dl_ref,
         k_ref, v_ref, sb_ref, S_ref, kdb_ref, cdb_ref) = refs
    b = pl.program_id(0)
    i = pl.program_id(1)

    @pl.when((b == 0) & (i == 0))
    def _():
        r = lax.broadcasted_iota(jnp.int32, (CHUNK, DK), 0).astype(F32)
        for hd in range(HEADS):
            lgb = _log_decay(dl_ref, BWD, hd)
            kdb_ref[hd] = jnp.exp(r * lgb[:, :DK]) * DK ** -0.5
            cdb_ref[hd] = jnp.broadcast_to(jnp.exp(float(CHUNK) * lgb), (8, DV))

    @pl.when(i == 0)
    def _():
        S_ref[...] = jnp.zeros_like(S_ref)

    chunks = [slice(ci * CHUNK, (ci + 1) * CHUNK) for ci in reversed(range(k_ref.shape[1] // CHUNK))]

    def tile(src_ref):
        ks, vs = [], []
        for rows in chunks:
            if normalize:
                h = _rms(src_ref[0, rows, :], g_ref[...]).astype(BF16)
                h_ref[0, rows, :] = h
            else:
                h = src_ref[0, rows, :]
            ks.append(_dot(h, wk_ref[...]))
            v = jnp.concatenate([_dot(h, wv0_ref[...]), _dot(h, wv1_ref[...])], axis=-1).astype(BF16)
            v_ref[0, rows, :] = v
            vs.append(v)
        for rows, k, v in zip(chunks, ks, vs):
            ci = rows.start // CHUNK
            c, s = _rotary_chunk(bt_ref, ot_ref, ci)
            for hd in range(HEADS):
                kh = _rot(k[:, hd * DK:(hd + 1) * DK], c, s)
                k_ref[0, rows, hd * DK:(hd + 1) * DK] = kh.astype(BF16)
                S = S_ref[hd]
                sb_ref[0, ci, hd] = S.astype(BF16)
                upd = _dot_tn((kh * kdb_ref[hd]).astype(BF16), v[:, hd * DV:(hd + 1) * DV])
                S_ref[hd] = S * cdb_ref[hd][0:1, :] + upd

    if normalize:
        pl.when(b == 0)(lambda: tile(xp_ref))
        pl.when(b != 0)(lambda: tile(xs_ref))
    else:
        tile(x_ref)


def _mixer_kernel(*refs, seq_len, paired):
    xs_ref = None
    if paired:
        x_ref, xs_ref, *refs = refs
    else:
        x_ref, *refs = refs
    (h_ref, hp_ref, hn_ref, k_ref, v_ref, sb_ref, bt_ref, ot_ref, dl_ref,
     win_ref, pw_ref, ps_ref, wpo_ref, wro_ref, wo_ref, o_ref,
     S_ref, mask_ref, qd_ref, kdf_ref, cdf_ref, hext_ref, xa_ref, wpool_ref, scm_ref, qdec_ref) = refs
    b = pl.program_id(0)
    i = pl.program_id(1)
    nt = pl.num_programs(1)
    T = h_ref.shape[1]
    C = CHUNK
    chunks = [slice(ci * C, (ci + 1) * C) for ci in range(T // C)]
    heads = [slice(hd * DK, (hd + 1) * DK) for hd in range(HEADS)]
    vheads = [slice(hd * DV, (hd + 1) * DV) for hd in range(HEADS)]

    @pl.when((b == 0) & (i == 0))
    def _():
        ri = lax.broadcasted_iota(jnp.int32, (C, C), 0).astype(F32)
        ci = lax.broadcasted_iota(jnp.int32, (C, C), 1).astype(F32)
        r = lax.broadcasted_iota(jnp.int32, (C, DK), 0).astype(F32)
        for hd in range(HEADS):
            lgf = _log_decay(dl_ref, FWD, hd)
            lgb = _log_decay(dl_ref, BWD, hd)
            dfw = ri - ci
            m = jnp.where(dfw >= 0.0,
                          jnp.exp(lgf[:, :C] * jnp.maximum(dfw, 0.0)),
                          jnp.exp(lgb[:, :C] * jnp.maximum(-dfw, 0.0)))
            mask_ref[hd] = m * DK ** -0.5
            qd_ref[hd, :, 0:DK] = jnp.exp((r + 1.0) * lgf[:, :DK])
            qd_ref[hd, :, DK:2 * DK] = jnp.exp((C - r) * lgb[:, :DK])
            kdf_ref[hd] = jnp.exp((C - 1.0 - r) * lgf[:, :DK]) * DK ** -0.5
            cdf_ref[hd] = jnp.broadcast_to(jnp.exp(float(C) * lgf), (8, DV))
        for gi in range(len(POOL_WINDOWS)):
            cols = slice(gi * POOL_GROUP_DIM, (gi + 1) * POOL_GROUP_DIM)
            wpool_ref[cols, :] = jnp.dot(pw_ref[gi] * ps_ref[:, cols], wpo_ref[cols, :],
                                         precision=lax.Precision.HIGHEST,
                                         preferred_element_type=F32).astype(BF16)

    @pl.when(i == 0)
    def _():
        S_ref[...] = jnp.zeros_like(S_ref)

    h = h_ref[0]
    q = _dot(h, win_ref[:, OFF_Q:OFF_K])

    hext_ref[0:HALO, :] = jnp.where(i > 0, hp_ref[0], jnp.zeros_like(hp_ref[0]))
    hext_ref[HALO:HALO + T, :] = h
    hext_ref[HALO + T:, :] = jnp.where(i < nt - 1, hn_ref[0], jnp.zeros_like(hn_ref[0]))
    xa_ref[...] = _dot(hext_ref[...], win_ref[:, 0:POOL_WIDTH])

    rot = [_rotary_chunk(bt_ref, ot_ref, ci) for ci in range(len(chunks))]
    dyn0 = jnp.minimum(i, 0)
    for ci, rows in enumerate(chunks):
        for hd in range(HEADS):
            qh = _rot(q[rows, heads[hd]], *rot[ci])
            qdec_ref[ci + dyn0, hd] = (jnp.concatenate([qh, qh], axis=-1) * qd_ref[hd]).astype(BF16)
            sc = _dot_nt(qh.astype(BF16), k_ref[0, rows, heads[hd]])
            scm_ref[ci + dyn0, hd] = (sc * mask_ref[hd]).astype(BF16)

    xg = _dot(h_ref[0], win_ref[:, OFF_G:OFF_GATE])

    ys = []
    for ci, rows in enumerate(chunks):
        ys.append([])
        for hd in range(HEADS):
            vb = v_ref[0, rows, vheads[hd]]
            S = S_ref[hd]
            states = jnp.concatenate([S.astype(BF16), sb_ref[0, ci, hd]], axis=0)
            ys[ci].append(_dot(scm_ref[ci + dyn0, hd], vb) + _dot(qdec_ref[ci + dyn0, hd], states))
            kf = (k_ref[0, rows, heads[hd]].astype(F32) * kdf_ref[hd]).astype(BF16)
            S_ref[hd] = S * cdf_ref[hd][0:1, :] + _dot_tn(kf, vb)

    gz = _dot(h_ref[0], win_ref[:, OFF_GATE:])

    tglob = i * T + lax.broadcasted_iota(jnp.int32, (T, POOL_GROUP_DIM), 0)
    diffs = []
    for gi, w in enumerate(POOL_WINDOWS):
        cols = slice(gi * POOL_GROUP_DIM, (gi + 1) * POOL_GROUP_DIM)
        if w <= 4:
            acc = xa_ref[HALO - w // 2:HALO - w // 2 + T, cols]
            for kk in range(-w // 2 + 1, w // 2):
                acc = acc + xa_ref[HALO + kk:HALO + kk + T, cols]
        else:
            n = T + HALO
            part = xa_ref[HALO // 2:HALO // 2 + n, cols]
            for k in (1, 2, 4, 8)[:w.bit_length() - 1]:
                part = part + pltpu.roll(part, n - k, 0)
            shift = HALO // 2 - w // 2
            acc = (pltpu.roll(part, n - shift, 0) if shift else part)[0:T]
        lo = jnp.maximum(tglob - w // 2, 0)
        hi = jnp.minimum(tglob + w // 2, seq_len)
        mean = acc / (hi - lo).astype(F32)
        diffs.append((mean - xa_ref[HALO:HALO + T, cols]).astype(BF16))
    a = _dot(jnp.concatenate(diffs, axis=-1), wpool_ref[...])

    rs = []
    for ci, rows in enumerate(chunks):
        rin = []
        for hd in range(HEADS):
            y = ys[ci][hd]
            yc = y - jnp.mean(y, axis=-1, keepdims=True)
            yn = yc * lax.rsqrt(jnp.mean(yc * yc, axis=-1, keepdims=True) + GN_EPS)
            gh = xg[rows, vheads[hd]]
            rin.append((yn * _silu_of_half(gh)).astype(BF16))
        rs.append(_dot(jnp.concatenate(rin, axis=-1), wro_ref[...]))

    for rows, r in zip(chunks, rs):
        th = jnp.tanh(gz[rows])
        merged = (th[:, :D_MODEL] + 1.0) * a[rows] + (th[:, D_MODEL:] + 1.0) * r
        x = x_ref[0, rows, :]
        if paired:
            x = jnp.where(b == 0, x, xs_ref[0, rows, :])
        o_ref[0, rows, :] = x + _dot(merged.astype(BF16), wo_ref[...])


def _ffn_kernel(*refs, final_norm):
    if final_norm:
        x_ref, p_ref, gf_ref, wfi_ref, wfo_ref, gp_ref, wpg_ref, wpp_ref, gout_ref, o_ref = refs
    else:
        (x_ref, p_ref, ps_ref, gf_ref, wfi_ref, wfo_ref, gp_ref, wpg_ref, wpp_ref, gout_ref,
         o_ref, hn_ref) = refs
    b = pl.program_id(0)
    rpg = x_ref.shape[1] // FFN_GROUPS
    groups = [slice(gi * rpg, (gi + 1) * rpg) for gi in range(FFN_GROUPS)]
    xs = [x_ref[0, rows, :] for rows in groups]
    gt, up, pp = [], [], []
    for x, rows in zip(xs, groups):
        h2 = _rms(x, gf_ref[...]).astype(BF16)
        gt.append(_dot(h2, wfi_ref[:, :D_FF]))
        up.append(_dot(h2, wfi_ref[:, D_FF:]))
        p = p_ref[0, rows, :]
        if not final_norm:
            p = jnp.where(b == 0, p, ps_ref[0, rows, :])
        pp.append(_dot(p.astype(BF16), wpp_ref[...]))
    xs = [x + _dot((_silu_of_half(g) * u).astype(BF16), wfo_ref[...])
          for x, g, u in zip(xs, gt, up)]
    pg = [_sigmoid_of_half(_dot(_rms(x, gp_ref[...]).astype(BF16), wpg_ref[...])) for x in xs]
    for x, rows, proj, gate in zip(xs, groups, pp, pg):
        x = x + proj * gate
        if final_norm:
            o_ref[0, rows, :] = _rms(x, gout_ref[...])
        else:
            o_ref[0, rows, :] = x
            hn_ref[0, rows, :] = _rms(x, gout_ref[...]).astype(BF16)


def _layer_spec(arr, layer):
    nd = arr.ndim - 1
    return pl.BlockSpec((None,) + arr.shape[1:], lambda b, i: (layer,) + (0,) * nd,
                        pipeline_mode=pl.Buffered(1))


def _const_spec(arr):
    nd = arr.ndim
    return pl.BlockSpec(arr.shape, lambda b, i: (0,) * nd, pipeline_mode=pl.Buffered(1))


def _pair_specs(block, nt, tile_of, lead=()):
    first, last = tile_of(0), tile_of(nt - 1)
    prompt = pl.BlockSpec(block, lambda b, i: lead + (0, jnp.where(b == 0, tile_of(i), last), 0))
    sample = pl.BlockSpec(block, lambda b, i: lead + (jnp.maximum(b - 1, 0),
                                                      jnp.where(b == 0, first, tile_of(i)), 0))
    return [prompt, sample]


def _params(semantics):
    return pltpu.CompilerParams(dimension_semantics=semantics, vmem_limit_bytes=VMEM_LIMIT)


def _kv(x, layer, g_mix, w_in, rot_base, rot_off, dl, normalize):
    B = x[0].shape[0] + x[1].shape[0] if normalize else x.shape[0]
    _, L, D = x[0].shape if normalize else x.shape
    T = T_KV
    nt = L // T
    cpt = T // CHUNK
    rev = lambda b, i: (b, nt - 1 - i, 0)
    wcols = lambda off: pl.BlockSpec((None, D, QK_WIDTH), lambda b, i: (layer, 0, off // QK_WIDTH),
                                     pipeline_mode=pl.Buffered(1))
    out_specs, out_shape = [], []
    if normalize:
        in_specs = _pair_specs((1, T, D), nt, lambda i: nt - 1 - i) + [_layer_spec(g_mix, layer)]
        args = [x[0], x[1], g_mix]
        out_specs.append(pl.BlockSpec((1, T, D), rev))
        out_shape.append(jax.ShapeDtypeStruct((B, L, D), BF16))
    else:
        in_specs = [pl.BlockSpec((1, T, D), rev)]
        args = [x]
    in_specs += [
        wcols(OFF_K),
        wcols(OFF_V),
        wcols(OFF_V + QK_WIDTH),
        pl.BlockSpec((cpt, 3, DK), lambda b, i: (nt - 1 - i, 0, 0)),
        _const_spec(rot_off),
        _layer_spec(dl, layer),
    ]
    args += [w_in, w_in, w_in, rot_base, rot_off, dl]
    out_specs += [
        pl.BlockSpec((1, T, QK_WIDTH), rev),
        pl.BlockSpec((1, T, V_WIDTH), rev),
        pl.BlockSpec((1, cpt, HEADS, DK, DV), lambda b, i: (b, nt - 1 - i, 0, 0, 0)),
    ]
    out_shape += [
        jax.ShapeDtypeStruct((B, L, QK_WIDTH), BF16),
        jax.ShapeDtypeStruct((B, L, V_WIDTH), BF16),
        jax.ShapeDtypeStruct((B, L // CHUNK, HEADS, DK, DV), BF16),
    ]
    return pl.pallas_call(
        functools.partial(_kv_kernel, normalize=normalize),
        grid=(B, nt),
        in_specs=in_specs,
        out_specs=out_specs,
        out_shape=out_shape,
        scratch_shapes=[
            pltpu.VMEM((HEADS, DK, DV), F32),
            pltpu.VMEM((HEADS, CHUNK, DK), F32),
            pltpu.VMEM((HEADS, 8, DV), F32),
        ],
        compiler_params=_params(("arbitrary", "arbitrary")),
        name="kv",
    )(*args)


def _mixer(x, h, k, v, sb, layer, w_in, pool_w, pool_scale, w_pool_out, w_ret_out, w_o,
           rot_base, rot_off, dl):
    paired = isinstance(x, tuple)
    B, L, D = h.shape
    T = T_MIX
    nt = L // T
    cpt = T // CHUNK
    hb = T // HALO
    nhb = L // HALO
    tile = lambda b, i: (b, i, 0)
    return pl.pallas_call(
        functools.partial(_mixer_kernel, seq_len=L, paired=paired),
        grid=(B, nt),
        in_specs=(_pair_specs((1, T, D), nt, lambda i: i) if paired
                  else [pl.BlockSpec((1, T, D), tile)]) + [
            pl.BlockSpec((1, T, D), tile),
            pl.BlockSpec((1, HALO, D), lambda b, i: (b, jnp.maximum(i * hb - 1, 0), 0)),
            pl.BlockSpec((1, HALO, D), lambda b, i: (b, jnp.minimum((i + 1) * hb, nhb - 1), 0)),
            pl.BlockSpec((1, T, QK_WIDTH), tile),
            pl.BlockSpec((1, T, V_WIDTH), tile),
            pl.BlockSpec((1, cpt, HEADS, DK, DV), lambda b, i: (b, i, 0, 0, 0)),
            pl.BlockSpec((cpt, 3, DK), lambda b, i: (i, 0, 0)),
            _const_spec(rot_off),
            _layer_spec(dl, layer),
            _layer_spec(w_in, layer),
            _layer_spec(pool_w, layer),
            _layer_spec(pool_scale, layer),
            _layer_spec(w_pool_out, layer),
            _layer_spec(w_ret_out, layer),
            _layer_spec(w_o, layer),
        ],
        out_specs=pl.BlockSpec((1, T, D), tile),
        out_shape=jax.ShapeDtypeStruct((B, L, D), F32),
        scratch_shapes=[
            pltpu.VMEM((HEADS, DK, DV), F32),
            pltpu.VMEM((HEADS, CHUNK, CHUNK), F32),
            pltpu.VMEM((HEADS, CHUNK, 2 * DK), F32),
            pltpu.VMEM((HEADS, CHUNK, DK), F32),
            pltpu.VMEM((HEADS, 8, DV), F32),
            pltpu.VMEM((T + 2 * HALO, D), BF16),
            pltpu.VMEM((T + 2 * HALO, POOL_WIDTH), F32),
            pltpu.VMEM((POOL_WIDTH, D), BF16),
            pltpu.VMEM((cpt, HEADS, CHUNK, CHUNK), BF16),
            pltpu.VMEM((cpt, HEADS, CHUNK, 2 * DK), BF16),
        ],
        compiler_params=_params(("arbitrary", "arbitrary")),
        name="mixer",
    )(*(x if paired else (x,)), h, h, h, k, v, sb, rot_base, rot_off, dl, w_in, pool_w, pool_scale,
      w_pool_out, w_ret_out, w_o)


def _ffn(x, p, layer, g_ffn, w_ffn_in, w_ffn_out, g_ple, w_ple_gate, w_ple_proj, g_out, final_norm,
         batch_offset=0):
    _, L, D = x.shape
    T = T_FFN
    nt = L // T
    tile = lambda b, i: (b, i, 0)
    if final_norm:
        B = p.shape[1]
        p_specs = [pl.BlockSpec((None, 1, T, PLE_DIM), lambda b, i: (layer, b, i, 0))]
        p_args = [p]
        out_specs = pl.BlockSpec((1, T, D), tile)
        out_shape = jax.ShapeDtypeStruct((B, L, D), F32)
    else:
        B = x.shape[0]
        p_specs = _pair_specs((None, 1, T, PLE_DIM), nt, lambda i: i, lead=(layer,))
        p_args = list(p)
        out_specs = [pl.BlockSpec((1, T, D), tile)] * 2
        out_shape = [jax.ShapeDtypeStruct((B, L, D), F32), jax.ShapeDtypeStruct((B, L, D), BF16)]
    return pl.pallas_call(
        functools.partial(_ffn_kernel, final_norm=final_norm),
        grid=(B, nt),
        in_specs=[pl.BlockSpec((1, T, D), lambda b, i: (b + batch_offset, i, 0))] + p_specs + [
            _layer_spec(g_ffn, layer),
            _layer_spec(w_ffn_in, layer),
            _layer_spec(w_ffn_out, layer),
            _layer_spec(g_ple, layer),
            _layer_spec(w_ple_gate, layer),
            _layer_spec(w_ple_proj, layer),
            _const_spec(g_out),
        ],
        out_specs=out_specs,
        out_shape=out_shape,
        compiler_params=_params(("arbitrary", "arbitrary")),
        name="ffn",
    )(x, *p_args, g_ffn, w_ffn_in, w_ffn_out, g_ple, w_ple_gate, w_ple_proj, g_out)


def _rotary_tables(L):
    half = DK // 2
    inv = 1.0 / (ROPE_BASE ** (jnp.arange(half, dtype=F32) / half))
    base = (jnp.arange(L // CHUNK) * CHUNK).astype(F32)[:, None] * inv
    off = jnp.arange(CHUNK).astype(F32)[:, None] * inv
    planes = lambda c, s: jnp.stack([jnp.concatenate([c, c], -1), jnp.concatenate([s, s], -1),
                                     jnp.concatenate([-s, s], -1)])
    rot_base = jnp.transpose(planes(jnp.cos(base), jnp.sin(base)), (1, 0, 2))
    return rot_base, planes(jnp.cos(off), jnp.sin(off))


def kernel(x_prompt, x_sample, p_prompt, p_sample, g_mix, w_in, pool_w, pool_scale, decay_logit,
           w_pool_out, w_ret_out, w_o, g_ffn, w_ffn_in, w_ffn_out, g_ple, w_ple_gate, w_ple_proj,
           g_final):
    depth = w_in.shape[0]
    L = x_prompt.shape[1]
    rot_base, rot_off = _rotary_tables(L)
    halved = lambda width, lo, hi: jnp.where((jnp.arange(width) >= lo) & (jnp.arange(width) < hi),
                                             HALF, 1.0).astype(F32)
    w_in_b = (w_in * halved(w_in.shape[-1], OFF_G, w_in.shape[-1])).astype(BF16)
    w_ret_out_b = w_ret_out.astype(BF16)
    w_o_b = (w_o * HALF).astype(BF16)
    w_ffn_in_b = (w_ffn_in * halved(2 * D_FF, 0, D_FF)).astype(BF16)
    w_ffn_out_b = w_ffn_out.astype(BF16)
    w_ple_gate_b = (w_ple_gate * HALF).astype(BF16)
    w_ple_proj_b = w_ple_proj.astype(BF16)
    dl = jnp.broadcast_to(decay_logit.astype(F32)[..., None, None], decay_logit.shape + (1, DV))
    rows = lambda v: v.reshape(v.shape[0], 1, v.shape[1])
    g_mix_r, g_ffn_r, g_ple_r, pool_scale_r = rows(g_mix), rows(g_ffn), rows(g_ple), rows(pool_scale)
    g_final_r = g_final.reshape(1, -1)

    x = (x_prompt, x_sample)
    p = (p_prompt, p_sample)
    ffn_w = (g_ffn_r, w_ffn_in_b, w_ffn_out_b, g_ple_r, w_ple_gate_b, w_ple_proj_b)
    for l in range(depth):
        if l == 0:
            h, k, v, sb = _kv(x, l, g_mix_r, w_in_b, rot_base, rot_off, dl, True)
        else:
            k, v, sb = _kv(h, l, g_mix_r, w_in_b, rot_base, rot_off, dl, False)
        x = _mixer(x, h, k, v, sb, l, w_in_b, pool_w, pool_scale_r, w_pool_out,
                   w_ret_out_b, w_o_b, rot_base, rot_off, dl)
        if l < depth - 1:
            x, h = _ffn(x, p, l, *ffn_w, g_mix_r[l + 1], False)
    nb = x_prompt.shape[0]
    return (_ffn(x, p_prompt, depth - 1, *ffn_w, g_final_r, True),
            _ffn(x, p_sample, depth - 1, *ffn_w, g_final_r, True, batch_offset=nb))
```
